```python
import functools
import jax, jax.numpy as jnp
from jax import lax
import numpy as np

D_MODEL = 2048
BATCH = 16
SEQ = 2048
DEPTH = 2

CHUNK = 64
N_MIXERS = 2
N_SUBLAYERS = 3
RET_HEADS = 8
RET_QK_DIM = D_MODEL // RET_HEADS
RET_V_DIM = 2 * RET_QK_DIM
RET_VALUE_WIDTH = RET_HEADS * RET_V_DIM
RET_IN_WIDTH = 2 * D_MODEL + 2 * RET_VALUE_WIDTH
CONV_WIDTH = 31
D_FF = 4 * D_MODEL
ROPE_BASE = 10000.0
EPS = 1e-6
N_RET_LAYERS = (DEPTH + 1) // 2
N_CONV_LAYERS = DEPTH // 2

kernel_name = "hybrid_retention_conformer_macaron_adaln"


def rmsnorm(x, g):
    xf = x.astype(jnp.float32)
    y = xf * lax.rsqrt(jnp.mean(xf * xf, axis=-1, keepdims=True) + EPS)
    return (y * g.astype(jnp.float32)).astype(x.dtype)


def layernorm(x, g, b):
    xf = x.astype(jnp.float32)
    mu = jnp.mean(xf, axis=-1, keepdims=True)
    var = jnp.mean(jnp.square(xf - mu), axis=-1, keepdims=True)
    y = (xf - mu) * lax.rsqrt(var + EPS)
    return (y * g.astype(jnp.float32) + b.astype(jnp.float32)).astype(x.dtype)


def rope(t, pos):
    half = t.shape[-1] // 2
    inv = ROPE_BASE ** (-jnp.arange(half, dtype=jnp.float32) / half)
    ang = pos.astype(jnp.float32)[:, None] * inv[None, :]
    cos = jnp.cos(ang)[None, :, None, :]
    sin = jnp.sin(ang)[None, :, None, :]
    t1, t2 = t[..., :half], t[..., half:]
    return jnp.concatenate([t1 * cos - t2 * sin, t1 * sin + t2 * cos], axis=-1)


def swiglu(h, w_gate, w_up, w_down):
    return (jax.nn.silu(h @ w_gate) * (h @ w_up)) @ w_down


def retention(h, w_in, gn_g, w_out):
    B, S, _ = h.shape
    nc = S // CHUNK
    proj = h @ w_in
    q, k, v, g = jnp.split(proj, [D_MODEL, 2 * D_MODEL, 2 * D_MODEL + RET_VALUE_WIDTH], axis=-1)
    pos = jnp.arange(S)
    q = rope(q.reshape(B, S, RET_HEADS, RET_QK_DIM).astype(jnp.float32), pos)
    k = rope(k.reshape(B, S, RET_HEADS, RET_QK_DIM).astype(jnp.float32), pos) * (RET_QK_DIM ** -0.5)
    v = v.reshape(B, S, RET_HEADS, RET_V_DIM).astype(jnp.float32)

    def to_chunks(t):
        return t.reshape(B, nc, CHUNK, RET_HEADS, t.shape[-1]).transpose(1, 0, 3, 2, 4)

    qc, kc, vc = to_chunks(q), to_chunks(k), to_chunks(v)

    gamma = 1.0 - jnp.exp2(-5.0 - jnp.arange(RET_HEADS, dtype=jnp.float32))
    log_g = jnp.log(gamma)
    i = jnp.arange(CHUNK, dtype=jnp.float32)
    decay_intra = jnp.exp(log_g[:, None, None] * jnp.abs(i[:, None] - i[None, :]))
    q_decay = jnp.exp(log_g[:, None] * (i[None, :] + 1.0))
    k_decay = jnp.exp(log_g[:, None] * (CHUNK - 1.0 - i[None, :]))
    chunk_decay = jnp.exp(log_g * CHUNK)

    scores = jnp.einsum('nbhid,nbhjd->nbhij', qc, kc) * decay_intra
    intra = jnp.einsum('nbhij,nbhjv->nbhiv', scores, vc)

    def step(state, inp):
        qj, kj, vj = inp
        cross = jnp.einsum('bhid,bhdv->bhiv', qj * q_decay[None, :, :, None], state)
        state = state * chunk_decay[None, :, None, None] + jnp.einsum(
            'bhid,bhiv->bhdv', kj * k_decay[None, :, :, None], vj)
        return state, cross

    s0 = jnp.zeros((B, RET_HEADS, RET_QK_DIM, RET_V_DIM), jnp.float32)
    _, cross = lax.scan(step, s0, (qc, kc, vc))
    o = (intra + cross).transpose(1, 0, 3, 2, 4).reshape(B, S, RET_HEADS, RET_V_DIM)

    mu = jnp.mean(o, axis=-1, keepdims=True)
    var = jnp.mean(jnp.square(o - mu), axis=-1, keepdims=True)
    o = ((o - mu) * lax.rsqrt(var + EPS)).reshape(B, S, RET_VALUE_WIDTH) * gn_g.astype(jnp.float32)
    y = (jax.nn.silu(g.astype(jnp.float32)) * o).astype(h.dtype)
    return y @ w_out


def conv_module(h, w_in, b_in, dw, dw_b, ln_g, ln_b, w_out, b_out):
    a = h @ w_in + b_in
    u = a[..., :D_MODEL] * jax.nn.sigmoid(a[..., D_MODEL:])
    u = lax.conv_general_dilated(
        u, dw.reshape(CONV_WIDTH, 1, D_MODEL).astype(u.dtype),
        window_strides=(1,), padding=[(CONV_WIDTH - 1, 0)],
        dimension_numbers=('NWC', 'WIO', 'NWC'),
        feature_group_count=D_MODEL) + dw_b
    u = jax.nn.silu(layernorm(u, ln_g, ln_b))
    return u @ w_out + b_out


def sublayer(x, fn, mod, g_pre, g_post, res_w):
    shift, scale, gate = jnp.split(mod, 3, axis=-1)
    h = rmsnorm(x, g_pre) * (1.0 + scale[:, None, :]) + shift[:, None, :]
    y = rmsnorm(fn(h), g_post)
    return x + res_w * gate[:, None, :] * y


def setup_inputs(seed: int = 0) -> dict:
    key = jax.random.key(seed)
    ks = jax.random.split(key, 20)
    f32 = jnp.float32

    def nrm(k, shape, scale):
        return jax.random.normal(k, shape, f32) * scale

    D, F, VW = D_MODEL, D_FF, RET_VALUE_WIDTH
    return {
        "x": nrm(ks[0], (BATCH, SEQ, D), 1.0),
        "c": nrm(ks[1], (BATCH, D), 1.0),
        "ada_w": nrm(ks[2], (DEPTH, N_SUBLAYERS, D, 3 * D), 0.5 * D ** -0.5),
        "ada_b": nrm(ks[3], (DEPTH, N_SUBLAYERS, 3 * D), 0.01),
        "norm_pre": 1.0 + nrm(ks[4], (DEPTH, N_SUBLAYERS, D), 0.05),
        "norm_post": 1.0 + nrm(ks[5], (DEPTH, N_SUBLAYERS, D), 0.05),
        "ffn_w_gate": nrm(ks[6], (DEPTH, 2, D, F), D ** -0.5),
        "ffn_w_up": nrm(ks[7], (DEPTH, 2, D, F), D ** -0.5),
        "ffn_w_down": nrm(ks[8], (DEPTH, 2, F, D), F ** -0.5),
        "ret_w_in": nrm(ks[9], (N_RET_LAYERS, D, RET_IN_WIDTH), D ** -0.5),
        "ret_gn": 1.0 + nrm(ks[10], (N_RET_LAYERS, VW), 0.05),
        "ret_w_out": nrm(ks[11], (N_RET_LAYERS, VW, D), VW ** -0.5),
        "conv_w_in": nrm(ks[12], (N_CONV_LAYERS, D, 2 * D), D ** -0.5),
        "conv_b_in": nrm(ks[13], (N_CONV_LAYERS, 2 * D), 0.01),
        "conv_dw": nrm(ks[14], (N_CONV_LAYERS, CONV_WIDTH, D), CONV_WIDTH ** -0.5),
        "conv_dw_b": nrm(ks[15], (N_CONV_LAYERS, D), 0.01),
        "conv_ln_g": 1.0 + nrm(ks[16], (N_CONV_LAYERS, D), 0.05),
        "conv_ln_b": nrm(ks[17], (N_CONV_LAYERS, D), 0.01),
        "conv_w_out": nrm(ks[18], (N_CONV_LAYERS, D, D), D ** -0.5),
        "conv_b_out": nrm(ks[19], (N_CONV_LAYERS, D), 0.01),
    }


def reference(x, c, ada_w, ada_b, norm_pre, norm_post, ffn_w_gate, ffn_w_up, ffn_w_down,
              ret_w_in, ret_gn, ret_w_out, conv_w_in, conv_b_in, conv_dw, conv_dw_b,
              conv_ln_g, conv_ln_b, conv_w_out, conv_b_out):
    cs = jax.nn.silu(c)
    for i in range(DEPTH):
        mods = jnp.einsum('bd,jde->jbe', cs, ada_w[i]) + ada_b[i][:, None, :]
        ffn1 = functools.partial(swiglu, w_gate=ffn_w_gate[i, 0], w_up=ffn_w_up[i, 0], w_down=ffn_w_down[i, 0])
        ffn2 = functools.partial(swiglu, w_gate=ffn_w_gate[i, 1], w_up=ffn_w_up[i, 1], w_down=ffn_w_down[i, 1])
        r = i // N_MIXERS
        if i % N_MIXERS == 0:
            mixer = functools.partial(retention, w_in=ret_w_in[r], gn_g=ret_gn[r], w_out=ret_w_out[r])
        else:
            mixer = functools.partial(conv_module, w_in=conv_w_in[r], b_in=conv_b_in[r], dw=conv_dw[r],
                                      dw_b=conv_dw_b[r], ln_g=conv_ln_g[r], ln_b=conv_ln_b[r],
                                      w_out=conv_w_out[r], b_out=conv_b_out[r])
        x = sublayer(x, ffn1, mods[0], norm_pre[i, 0], norm_post[i, 0], 0.5)
        x = sublayer(x, mixer, mods[1], norm_pre[i, 1], norm_post[i, 1], 1.0)
        x = sublayer(x, ffn2, mods[2], norm_pre[i, 2], norm_post[i, 2], 0.5)
    return x
```

```python
import functools

import jax
import jax.numpy as jnp
from jax import lax
from jax.experimental import pallas as pl
from jax.experimental.pallas import tpu as pltpu

EPS = 1e-6
RET_HEADS = 8
CHUNK = 64
CONV_WIDTH = 31
ROPE_BASE = 10000.0
N_SUBLAYERS = 3

F32 = jnp.float32
BF16 = jnp.bfloat16

V7X_VMEM_LIMIT_BYTES = 56 * 1024 * 1024
CONV_HALO = 32
CONV_ROW_CHUNK = 64
CONV_LANE_CHUNK = 256


def _tiles(m, s, d, f):
    dk = d // RET_HEADS
    return dict(
        bm=min(512, s),
        bf=min(512, f),
        bn=min(1024, d),
        bk=min(1024, 2 * d),
        tr=min(256, s),
        ada_bn=min(1024, 3 * d),
        dk=dk,
    )


def _params(sem):
    return pltpu.CompilerParams(dimension_semantics=sem, vmem_limit_bytes=V7X_VMEM_LIMIT_BYTES)


def _premod(x, mod_ref, g_pre, d):
    shift = mod_ref[0, :, 0:d]
    scale = mod_ref[0, :, d:2 * d]
    y = x * lax.rsqrt(jnp.mean(x * x, axis=-1, keepdims=True) + EPS) * g_pre
    return y * (1.0 + scale) + shift


def _post_residual(z, x, mod_ref, g_post, res_w, d):
    gate = mod_ref[0, :, 2 * d:3 * d]
    y = z * lax.rsqrt(jnp.mean(z * z, axis=-1, keepdims=True) + EPS) * g_post
    return x + (res_w * gate) * y


def _silu(v):
    return v * jax.nn.sigmoid(v)


def _ada_kernel(c_ref, w_ref, b_ref, o_ref):
    cs = _silu(c_ref[...]).astype(BF16)
    o_ref[0] = jnp.dot(cs, w_ref[0].astype(BF16), preferred_element_type=F32) + b_ref[0]


def _ada_mods(c, ada_w, ada_b, t):
    n, d, e = ada_w.shape
    b = c.shape[0]
    bn = t["ada_bn"]
    return pl.pallas_call(
        _ada_kernel,
        grid=(n, e // bn),
        in_specs=[
            pl.BlockSpec((b, d), lambda i, j: (0, 0)),
            pl.BlockSpec((1, d, bn), lambda i, j: (i, 0, j)),
            pl.BlockSpec((1, 1, bn), lambda i, j: (i, 0, j)),
        ],
        out_specs=pl.BlockSpec((1, b, bn), lambda i, j: (i, 0, j)),
        out_shape=jax.ShapeDtypeStruct((n, b, e), F32),
        compiler_params=_params(("arbitrary", "arbitrary")),
        name="ada_mods",
    )(c, ada_w, ada_b)


def _ffn_kernel(x_ref, mod_ref, gpre_ref, gpost_ref, wg_ref, wu_ref, wd_ref, o_ref, h_ref, acc_ref,
                *, res_w, d):
    f = pl.program_id(1)

    @pl.when(f == 0)
    def _():
        h_ref[...] = _premod(x_ref[...], mod_ref, gpre_ref[...], d).astype(BF16)
        acc_ref[...] = jnp.zeros_like(acc_ref)

    h = h_ref[...]
    hg = jnp.dot(h, wg_ref[...], preferred_element_type=F32)
    hu = jnp.dot(h, wu_ref[...], preferred_element_type=F32)
    a = (_silu(hg) * hu).astype(BF16)
    acc_ref[...] += jnp.dot(a, wd_ref[...], preferred_element_type=F32)

    @pl.when(f == pl.num_programs(1) - 1)
    def _():
        o_ref[...] = _post_residual(acc_ref[...], x_ref[...], mod_ref, gpost_ref[...], res_w, d)


def _ffn_sublayer(x, mod, g_pre, g_post, wg, wu, wd, res_w, s, t):
    m, d = x.shape
    f = wg.shape[1]
    bm, bf = t["bm"], t["bf"]
    tps = s // bm
    return pl.pallas_call(
        functools.partial(_ffn_kernel, res_w=res_w, d=d),
        grid=(m // bm, f // bf),
        in_specs=[
            pl.BlockSpec((bm, d), lambda i, j: (i, 0)),
            pl.BlockSpec((1, 1, 3 * d), lambda i, j: (i // tps, 0, 0)),
            pl.BlockSpec((1, d), lambda i, j: (0, 0)),
            pl.BlockSpec((1, d), lambda i, j: (0, 0)),
            pl.BlockSpec((d, bf), lambda i, j: (0, j)),
            pl.BlockSpec((d, bf), lambda i, j: (0, j)),
            pl.BlockSpec((bf, d), lambda i, j: (j, 0)),
        ],
        out_specs=pl.BlockSpec((bm, d), lambda i, j: (i, 0)),
        out_shape=jax.ShapeDtypeStruct((m, d), F32),
        scratch_shapes=[pltpu.VMEM((bm, d), BF16), pltpu.VMEM((bm, d), F32)],
        compiler_params=_params(("parallel", "arbitrary")),
        name="ffn_sublayer",
    )(x, mod, g_pre, g_post, wg, wu, wd)


def _qkvg_kernel(x_ref, mod_ref, gpre_ref, cos_ref, sin_ref, w_ref, qk_ref, v_ref, g_ref, h_ref,
                 *, d, dk, n_q, n_qk, n_v):
    j = pl.program_id(1)

    @pl.when(j == 0)
    def _():
        h_ref[...] = _premod(x_ref[...], mod_ref, gpre_ref[...], d).astype(BF16)

    y = jnp.dot(h_ref[...], w_ref[...], preferred_element_type=F32)
    half = dk // 2

    @pl.when(j < n_qk)
    def _():
        sc = jnp.where(j >= n_q, dk ** -0.5, 1.0).astype(F32)
        cos = cos_ref[...] * sc
        sin = sin_ref[...] * sc
        for hh in range(y.shape[1] // dk):
            t1 = y[:, hh * dk:hh * dk + half]
            t2 = y[:, hh * dk + half:(hh + 1) * dk]
            qk_ref[:, hh * dk:hh * dk + half] = (t1 * cos - t2 * sin).astype(BF16)
            qk_ref[:, hh * dk + half:(hh + 1) * dk] = (t1 * sin + t2 * cos).astype(BF16)

    @pl.when(jnp.logical_and(j >= n_qk, j < n_qk + n_v))
    def _():
        v_ref[...] = y.astype(BF16)

    @pl.when(j >= n_qk + n_v)
    def _():
        g_ref[...] = y


def _qkvg_proj(x, mod, g_pre, cos, sin, w_in, s, t):
    m, d = x.shape
    bm, bn, dk = t["bm"], t["bn"], t["dk"]
    tps = s // bm
    vw = 2 * d
    n_q, n_qk, n_v, n_g = d // bn, 2 * d // bn, vw // bn, vw // bn
    half = dk // 2
    return pl.pallas_call(
        functools.partial(_qkvg_kernel, d=d, dk=dk, n_q=n_q, n_qk=n_qk, n_v=n_v),
        grid=(m // bm, n_qk + n_v + n_g),
        in_specs=[
            pl.BlockSpec((bm, d), lambda i, j: (i, 0)),
            pl.BlockSpec((1, 1, 3 * d), lambda i, j: (i // tps, 0, 0)),
            pl.BlockSpec((1, d), lambda i, j: (0, 0)),
            pl.BlockSpec((bm, half), lambda i, j: (i % tps, 0)),
            pl.BlockSpec((bm, half), lambda i, j: (i % tps, 0)),
            pl.BlockSpec((d, bn), lambda i, j: (0, j)),
        ],
        out_specs=[
            pl.BlockSpec((bm, bn), lambda i, j: (i, jnp.minimum(j, n_qk - 1))),
            pl.BlockSpec((bm, bn), lambda i, j: (i, jnp.clip(j - n_qk, 0, n_v - 1))),
            pl.BlockSpec((bm, bn), lambda i, j: (i, jnp.clip(j - n_qk - n_v, 0, n_g - 1))),
        ],
        out_shape=[
            jax.ShapeDtypeStruct((m, 2 * d), BF16),
            jax.ShapeDtypeStruct((m, vw), BF16),
            jax.ShapeDtypeStruct((m, vw), F32),
        ],
        scratch_shapes=[pltpu.VMEM((bm, d), BF16)],
        compiler_params=_params(("parallel", "arbitrary")),
        name="ret_qkvg_proj",
    )(x, mod, g_pre, cos, sin, w_in)


def _retention_kernel(q_ref, k_ref, v_ref, g_ref, gn_ref, dmat_ref, tab_ref, o_ref, state_ref):
    @pl.when(pl.program_id(2) == 0)
    def _():
        state_ref[...] = jnp.zeros_like(state_ref)

    q = q_ref[...]
    k = k_ref[...]
    v = v_ref[...]
    tab = tab_ref[0]
    qdec, kdec, cdec = tab[:, 0:1], tab[:, 1:2], tab[0:1, 2:3]
    state = state_ref[...]

    scores = lax.dot_general(q, k, (((1,), (1,)), ((), ())), preferred_element_type=F32)
    p = (scores * dmat_ref[0]).astype(BF16)
    o = jnp.dot(p, v, preferred_element_type=F32)
    o = o + qdec * jnp.dot(q, state.astype(BF16), preferred_element_type=F32)

    kd = (k.astype(F32) * kdec).astype(BF16)
    state_ref[...] = state * cdec + lax.dot_general(
        kd, v, (((0,), (0,)), ((), ())), preferred_element_type=F32)

    mu = jnp.mean(o, axis=-1, keepdims=True)
    oc = o - mu
    var = jnp.mean(oc * oc, axis=-1, keepdims=True)
    on = oc * lax.rsqrt(var + EPS) * gn_ref[...]
    o_ref[...] = (_silu(g_ref[...]) * on).astype(BF16)


def _retention_core(qk, v, g, gn_g, dmat, tab, b, s, t):
    m = qk.shape[0]
    h = RET_HEADS
    tr = t["tr"]
    dk = qk.shape[1] // (2 * h)
    dv = v.shape[1] // h
    nt = s // tr
    row = lambda bi, hi, ti: bi * nt + ti
    return pl.pallas_call(
        _retention_kernel,
        grid=(b, h, nt),
        in_specs=[
            pl.BlockSpec((tr, dk), lambda bi, hi, ti: (row(bi, hi, ti), hi)),
            pl.BlockSpec((tr, dk), lambda bi, hi, ti: (row(bi, hi, ti), h + hi)),
            pl.BlockSpec((tr, dv), lambda bi, hi, ti: (row(bi, hi, ti), hi)),
            pl.BlockSpec((tr, dv), lambda bi, hi, ti: (row(bi, hi, ti), hi)),
            pl.BlockSpec((1, dv), lambda bi, hi, ti: (0, hi)),
            pl.BlockSpec((1, tr, tr), lambda bi, hi, ti: (hi, 0, 0)),
            pl.BlockSpec((1, tr, 128), lambda bi, hi, ti: (hi, 0, 0)),
        ],
        out_specs=pl.BlockSpec((tr, dv), lambda bi, hi, ti: (row(bi, hi, ti), hi)),
        out_shape=jax.ShapeDtypeStruct((m, h * dv), BF16),
        scratch_shapes=[pltpu.VMEM((dk, dv), F32)],
        compiler_params=_params(("parallel", "parallel", "arbitrary")),
        name="ret_core",
    )(qk, qk, v, g, gn_g, dmat, tab)


def _retention_tables(s, dk, tr):
    half = dk // 2
    inv = ROPE_BASE ** (-jnp.arange(half, dtype=F32) / half)
    ang = jnp.arange(s, dtype=F32)[:, None] * inv[None, :]
    gamma = 1.0 - jnp.exp2(-5.0 - jnp.arange(RET_HEADS, dtype=F32))
    log_g = jnp.log(gamma)
    r = jnp.arange(tr, dtype=F32)
    n, mm = r[:, None], r[None, :]
    cn, cm = jnp.floor(n / CHUNK), jnp.floor(mm / CHUNK)
    dmat = jnp.where(cm <= cn, jnp.exp(log_g[:, None, None] * jnp.abs(n - mm)), 0.0)
    tab = jnp.zeros((RET_HEADS, tr, 128), F32)
    tab = tab.at[:, :, 0].set(jnp.exp(log_g[:, None] * (r[None, :] + 1.0)))
    tab = tab.at[:, :, 1].set(jnp.exp(log_g[:, None] * (tr - 1.0 - r[None, :])))
    tab = tab.at[:, :, 2].set(jnp.exp(log_g * tr)[:, None])
    return jnp.cos(ang), jnp.sin(ang), dmat, tab


def _outproj_kernel(y_ref, x_ref, mod_ref, gpost_ref, w_ref, o_ref, acc_ref, *, res_w, d):
    k = pl.program_id(1)

    @pl.when(k == 0)
    def _():
        acc_ref[...] = jnp.zeros_like(acc_ref)

    acc_ref[...] += jnp.dot(y_ref[...], w_ref[...], preferred_element_type=F32)

    @pl.when(k == pl.num_programs(1) - 1)
    def _():
        o_ref[...] = _post_residual(acc_ref[...], x_ref[...], mod_ref, gpost_ref[...], res_w, d)


def _outproj_sublayer(y, x, mod, g_post, w, res_w, s, t):
    m, d = x.shape
    kk = y.shape[1]
    bm, bk = t["bm"], t["bk"]
    tps = s // bm
    return pl.pallas_call(
        functools.partial(_outproj_kernel, res_w=res_w, d=d),
        grid=(m // bm, kk // bk),
        in_specs=[
            pl.BlockSpec((bm, bk), lambda i, k: (i, k)),
            pl.BlockSpec((bm, d), lambda i, k: (i, 0)),
            pl.BlockSpec((1, 1, 3 * d), lambda i, k: (i // tps, 0, 0)),
            pl.BlockSpec((1, d), lambda i, k: (0, 0)),
            pl.BlockSpec((bk, d), lambda i, k: (k, 0)),
        ],
        out_specs=pl.BlockSpec((bm, d), lambda i, k: (i, 0)),
        out_shape=jax.ShapeDtypeStruct((m, d), F32),
        scratch_shapes=[pltpu.VMEM((bm, d), F32)],
        compiler_params=_params(("parallel", "arbitrary")),
        name="ret_outproj",
    )(y, x, mod, g_post, w)


def _glu_kernel(x_ref, mod_ref, gpre_ref, wa_ref, wb_ref, ba_ref, bb_ref, u_ref, h_ref, *, d):
    @pl.when(pl.program_id(1) == 0)
    def _():
        h_ref[...] = _premod(x_ref[...], mod_ref, gpre_ref[...], d).astype(BF16)

    h = h_ref[...]
    a = jnp.dot(h, wa_ref[...], preferred_element_type=F32) + ba_ref[...]
    b = jnp.dot(h, wb_ref[...], preferred_element_type=F32) + bb_ref[...]
    u_ref[...] = a * jax.nn.sigmoid(b)


def _glu_proj(x, mod, g_pre, w_in, b_in, s, t):
    m, d = x.shape
    bm, bn = t["bm"], min(t["bn"], 512)
    tps = s // bm
    nh = d // bn
    return pl.pallas_call(
        functools.partial(_glu_kernel, d=d),
        grid=(m // bm, nh),
        in_specs=[
            pl.BlockSpec((bm, d), lambda i, j: (i, 0)),
            pl.BlockSpec((1, 1, 3 * d), lambda i, j: (i // tps, 0, 0)),
            pl.BlockSpec((1, d), lambda i, j: (0, 0)),
            pl.BlockSpec((d, bn), lambda i, j: (0, j)),
            pl.BlockSpec((d, bn), lambda i, j: (0, nh + j)),
            pl.BlockSpec((1, bn), lambda i, j: (0, j)),
            pl.BlockSpec((1, bn), lambda i, j: (0, nh + j)),
        ],
        out_specs=pl.BlockSpec((bm, bn), lambda i, j: (i, j)),
        out_shape=jax.ShapeDtypeStruct((m, d), F32),
        scratch_shapes=[pltpu.VMEM((bm, d), BF16)],
        compiler_params=_params(("parallel", "arbitrary")),
        name="conv_glu_proj",
    )(x, mod, g_pre, w_in, w_in, b_in, b_in)


def _conv_kernel(ucur_ref, uprev_ref, x_ref, mod_ref, dw_ref, dwb_ref, lng_ref, lnb_ref, w_ref,
                 bout_ref, gpost_ref, o_ref, ubuf_ref, cbuf_ref, *, res_w, d, tps, rc, cc):
    bm = ucur_ref.shape[0]
    first = (pl.program_id(0) % tps) == 0
    ubuf_ref[0:CONV_HALO, :] = jnp.where(first, 0.0, uprev_ref[...])
    ubuf_ref[CONV_HALO:CONV_HALO + bm, :] = ucur_ref[...]
    off = CONV_HALO - (CONV_WIDTH - 1)

    def row_body(r, carry):
        base = pl.multiple_of(r * rc, rc)
        for c in range(d // cc):
            cols = slice(c * cc, (c + 1) * cc)
            win = ubuf_ref[pl.ds(base, rc + CONV_HALO), cols]
            acc = jnp.broadcast_to(dwb_ref[:, cols], (rc, cc))
            for phase in range(8):
                part = None
                for tap in range(CONV_WIDTH):
                    o = off + tap
                    if o % 8 == phase:
                        term = win[o:o + rc] * dw_ref[tap:tap + 1, cols]
                        part = term if part is None else part + term
                acc = acc + part
            cbuf_ref[pl.ds(base, rc), cols] = acc
        return carry

    lax.fori_loop(0, bm // rc, row_body, 0)

    cv = cbuf_ref[...]
    mu = jnp.mean(cv, axis=-1, keepdims=True)
    cz = cv - mu
    var = jnp.mean(cz * cz, axis=-1, keepdims=True)
    yn = cz * lax.rsqrt(var + EPS) * lng_ref[...] + lnb_ref[...]
    hc = _silu(yn).astype(BF16)
    z = jnp.dot(hc, w_ref[...], preferred_element_type=F32) + bout_ref[...]
    o_ref[...] = _post_residual(z, x_ref[...], mod_ref, gpost_ref[...], res_w, d)


def _conv_sublayer(u, x, mod, dw, dw_b, ln_g, ln_b, w_out, b_out, g_post, res_w, s, t):
    m, d = x.shape
    bm = t["bm"]
    tps = s // bm
    hb = bm // CONV_HALO
    rc, cc = min(CONV_ROW_CHUNK, bm), min(CONV_LANE_CHUNK, d)
    full = lambda i: (0, 0)
    return pl.pallas_call(
        functools.partial(_conv_kernel, res_w=res_w, d=d, tps=tps, rc=rc, cc=cc),
        grid=(m // bm,),
        in_specs=[
            pl.BlockSpec((bm, d), lambda i: (i, 0)),
            pl.BlockSpec((CONV_HALO, d), lambda i: (jnp.maximum(i * hb - 1, 0), 0)),
            pl.BlockSpec((bm, d), lambda i: (i, 0)),
            pl.BlockSpec((1, 1, 3 * d), lambda i: (i // tps, 0, 0)),
            pl.BlockSpec(dw.shape, full),
            pl.BlockSpec((1, d), full),
            pl.BlockSpec((1, d), full),
            pl.BlockSpec((1, d), full),
            pl.BlockSpec((d, d), full),
            pl.BlockSpec((1, d), full),
            pl.BlockSpec((1, d), full),
        ],
        out_specs=pl.BlockSpec((bm, d), lambda i: (i, 0)),
        out_shape=jax.ShapeDtypeStruct((m, d), F32),
        scratch_shapes=[pltpu.VMEM((CONV_HALO + bm, d), F32), pltpu.VMEM((bm, d), F32)],
        compiler_params=_params(("parallel",)),
        name="conv_sublayer",
    )(u, u, x, mod, dw, dw_b, ln_g, ln_b, w_out, b_out, g_post)


def kernel(x, c, ada_w, ada_b, norm_pre, norm_post, ffn_w_gate, ffn_w_up, ffn_w_down, ret_w_in, ret_gn,
           ret_w_out, conv_w_in, conv_b_in, conv_dw, conv_dw_b, conv_ln_g, conv_ln_b, conv_w_out, conv_b_out):
    b, s, d = x.shape
    depth = ada_w.shape[0]
    m = b * s
    f = ffn_w_gate.shape[-1]
    t = _tiles(m, s, d, f)
    assert s % t["bm"] == 0 and s % t["tr"] == 0 and t["tr"] % CHUNK == 0 and t["bm"] % CONV_HALO == 0
    assert CONV_HALO >= CONV_WIDTH - 1 and t["bn"] % t["dk"] == 0

    mods = _ada_mods(c, ada_w.reshape(depth * N_SUBLAYERS, d, 3 * d),
                     ada_b.reshape(depth * N_SUBLAYERS, 1, 3 * d), t)
    mods = mods.reshape(depth, N_SUBLAYERS, b, 1, 3 * d)
    row = lambda p: p.reshape(1, -1)
    cos, sin, dmat, tab = _retention_tables(s, t["dk"], t["tr"])
    dw_pad = CONV_HALO - CONV_WIDTH

    xf = x.reshape(m, d)
    for i in range(depth):
        r = i // 2
        ffn = lambda xin, j, w: _ffn_sublayer(
            xin, mods[i, j], row(norm_pre[i, j]), row(norm_post[i, j]), ffn_w_gate[i, w].astype(BF16),
            ffn_w_up[i, w].astype(BF16), ffn_w_down[i, w].astype(BF16), 0.5, s, t)
        xf = ffn(xf, 0, 0)
        if i % 2 == 0:
            qk, v, g = _qkvg_proj(xf, mods[i, 1], row(norm_pre[i, 1]), cos, sin, ret_w_in[r].astype(BF16), s, t)
            y = _retention_core(qk, v, g, row(ret_gn[r]), dmat, tab, b, s, t)
            xf = _outproj_sublayer(y, xf, mods[i, 1], row(norm_post[i, 1]), ret_w_out[r].astype(BF16), 1.0, s, t)
        else:
            u = _glu_proj(xf, mods[i, 1], row(norm_pre[i, 1]), conv_w_in[r].astype(BF16), row(conv_b_in[r]), s, t)
            xf = _conv_sublayer(u, xf, mods[i, 1], jnp.pad(conv_dw[r], ((0, dw_pad), (0, 0))), row(conv_dw_b[r]),
                                row(conv_ln_g[r]), row(conv_ln_b[r]), conv_w_out[r].astype(BF16),
                                row(conv_b_out[r]), row(norm_post[i, 1]), 1.0, s, t)
        xf = ffn(xf, 2, 1)
    return xf.reshape(b, s, d)
```

```python
import functools

import jax
import jax.numpy as jnp
from jax import lax
from jax.experimental import pallas as pl
from jax.experimental.pallas import tpu as pltpu

EPS = 1e-6
RET_HEADS = 8
CHUNK = 64
CONV_WIDTH = 31
ROPE_BASE = 10000.0
N_SUBLAYERS = 3

F32 = jnp.float32
BF16 = jnp.bfloat16

V7X_VMEM_LIMIT_BYTES = 60 * 1024 * 1024
LANES = 128
NORM_ROW_CHUNK = 16
CONV_HALO = 32
CONV_ROW_CHUNK = 64
CONV_LANE_CHUNK = 256


def _tiles(m, s, d, f):
    dk = d // RET_HEADS
    return dict(
        bm=min(512, s),
        bf=min(512, f),
        bn=min(512, d),
        bk=min(1024, 2 * d),
        tr=min(256, s),
        ada_bn=min(1024, 3 * d),
        dk=dk,
    )


def _params(sem):
    return pltpu.CompilerParams(dimension_semantics=sem, vmem_limit_bytes=V7X_VMEM_LIMIT_BYTES)


def _row_loop(n_rows, body, unroll):
    rc = min(NORM_ROW_CHUNK, n_rows)
    n = n_rows // rc

    def step(r, carry):
        body(pl.ds(pl.multiple_of(r * rc, rc), rc))
        return carry

    lax.fori_loop(0, n, step, 0, unroll=min(unroll, n))


def _inv_rms_rows(z_ref, inv_ref, bias_ref=None):
    def body(rows):
        zv = z_ref[rows, :]
        if bias_ref is not None:
            zv = zv + bias_ref[...]
        inv = lax.rsqrt(jnp.mean(zv * zv, axis=-1, keepdims=True) + EPS)
        inv_ref[rows, :] = jnp.broadcast_to(inv, (zv.shape[0], inv_ref.shape[1]))

    _row_loop(z_ref.shape[0], body, unroll=8)


def _premod_rows(x_ref, mod_ref, gpre_ref, h_ref, inv_ref, d):
    _inv_rms_rows(x_ref, inv_ref)
    shift = mod_ref[0, :, 0:d]
    gain = gpre_ref[...] * (1.0 + mod_ref[0, :, d:2 * d])
    lanes = inv_ref.shape[1]

    def body(rows):
        inv = inv_ref[rows, :]
        for j in range(d // lanes):
            cs = slice(j * lanes, (j + 1) * lanes)
            h_ref[rows, cs] = (x_ref[rows, cs] * inv * gain[:, cs] + shift[:, cs]).astype(h_ref.dtype)

    _row_loop(x_ref.shape[0], body, unroll=4)


def _post_residual_rows(z_ref, x_ref, mod_ref, gpost_ref, o_ref, inv_ref, res_w, d, bias_ref=None):
    _inv_rms_rows(z_ref, inv_ref, bias_ref)
    gain = (res_w * mod_ref[0, :, 2 * d:3 * d]) * gpost_ref[...]
    lanes = inv_ref.shape[1]

    def body(rows):
        inv = inv_ref[rows, :]
        for j in range(d // lanes):
            cs = slice(j * lanes, (j + 1) * lanes)
            zv = z_ref[rows, cs]
            if bias_ref is not None:
                zv = zv + bias_ref[:, cs]
            o_ref[rows, cs] = x_ref[rows, cs] + zv * inv * gain[:, cs]

    _row_loop(x_ref.shape[0], body, unroll=4)


def _silu(v):
    return v * jax.nn.sigmoid(v)


def _ada_kernel(c_ref, w_ref, b_ref, o_ref):
    cs = _silu(c_ref[...]).astype(BF16)
    o_ref[0] = jnp.dot(cs, w_ref[0].astype(BF16), preferred_element_type=F32) + b_ref[0]


def _ada_mods(c, ada_w, ada_b, t):
    n, d, e = ada_w.shape
    b = c.shape[0]
    bn = t["ada_bn"]
    return pl.pallas_call(
        _ada_kernel,
        grid=(n, e // bn),
        in_specs=[
            pl.BlockSpec((b, d), lambda i, j: (0, 0)),
            pl.BlockSpec((1, d, bn), lambda i, j: (i, 0, j)),
            pl.BlockSpec((1, 1, bn), lambda i, j: (i, 0, j)),
        ],
        out_specs=pl.BlockSpec((1, b, bn), lambda i, j: (i, 0, j)),
        out_shape=jax.ShapeDtypeStruct((n, b, e), F32),
        compiler_params=_params(("arbitrary", "arbitrary")),
        name="ada_mods",
    )(c, ada_w, ada_b)


def _ffn_kernel(x_ref, mod_ref, gpre_ref, gpost_ref, wg_ref, wu_ref, wd_ref, o_ref, h_ref, inv_ref,
                *, res_w, d):
    f = pl.program_id(1)

    @pl.when(f == 0)
    def _():
        _premod_rows(x_ref, mod_ref, gpre_ref, h_ref, inv_ref, d)
        o_ref[...] = jnp.zeros_like(o_ref)

    h = h_ref[...]
    hg = jnp.dot(h, wg_ref[...], preferred_element_type=F32)
    hu = jnp.dot(h, wu_ref[...], preferred_element_type=F32)
    a = (_silu(hg) * hu).astype(BF16)
    o_ref[...] += jnp.dot(a, wd_ref[...], preferred_element_type=F32)

    @pl.when(f == pl.num_programs(1) - 1)
    def _():
        _post_residual_rows(o_ref, x_ref, mod_ref, gpost_ref, o_ref, inv_ref, res_w, d)


def _ffn_sublayer(x, mod, g_pre, g_post, wg, wu, wd, layer, which, res_w, s, bm, bf):
    m, d = x.shape
    f = wg.shape[-1]
    tps = s // bm
    return pl.pallas_call(
        functools.partial(_ffn_kernel, res_w=res_w, d=d),
        grid=(m // bm, f // bf),
        in_specs=[
            pl.BlockSpec((bm, d), lambda i, j: (i, 0)),
            pl.BlockSpec((1, 1, 3 * d), lambda i, j: (i // tps, 0, 0)),
            pl.BlockSpec((1, d), lambda i, j: (0, 0)),
            pl.BlockSpec((1, d), lambda i, j: (0, 0)),
            pl.BlockSpec((None, None, d, bf), lambda i, j: (layer, which, 0, j)),
            pl.BlockSpec((None, None, d, bf), lambda i, j: (layer, which, 0, j)),
            pl.BlockSpec((None, None, bf, d), lambda i, j: (layer, which, j, 0)),
        ],
        out_specs=pl.BlockSpec((bm, d), lambda i, j: (i, 0)),
        out_shape=jax.ShapeDtypeStruct((m, d), F32),
        scratch_shapes=[pltpu.VMEM((bm, d), BF16), pltpu.VMEM((bm, LANES), F32)],
        compiler_params=_params(("parallel", "arbitrary")),
        name="ffn_sublayer",
    )(x, mod, g_pre, g_post, wg, wu, wd)


def _qkvg_kernel(x_ref, mod_ref, gpre_ref, cos_ref, sin_ref, wqk_ref, wv_ref, wg_ref, qk_ref, v_ref, g_ref,
                 h_ref, inv_ref, *, d, dk, n_q):
    j = pl.program_id(1)

    @pl.when(j == 0)
    def _():
        _premod_rows(x_ref, mod_ref, gpre_ref, h_ref, inv_ref, d)

    h = h_ref[...]
    y = jnp.dot(h, wqk_ref[...], preferred_element_type=F32)
    v_ref[...] = jnp.dot(h, wv_ref[...], preferred_element_type=F32).astype(BF16)
    g_ref[...] = jnp.dot(h, wg_ref[...], preferred_element_type=F32)

    half = dk // 2
    sc = jnp.where(j >= n_q, dk ** -0.5, 1.0).astype(F32)
    cos = cos_ref[...] * sc
    sin = sin_ref[...] * sc
    for hh in range(y.shape[1] // dk):
        t1 = y[:, hh * dk:hh * dk + half]
        t2 = y[:, hh * dk + half:(hh + 1) * dk]
        qk_ref[:, hh * dk:hh * dk + half] = (t1 * cos - t2 * sin).astype(BF16)
        qk_ref[:, hh * dk + half:(hh + 1) * dk] = (t1 * sin + t2 * cos).astype(BF16)


def _qkvg_proj(x, mod, g_pre, cos, sin, w_in, r, s, t):
    m, d = x.shape
    bm, bn, dk = t["bm"], t["bn"], t["dk"]
    tps = s // bm
    vw = 2 * d
    nj = vw // bn
    half = dk // 2
    col = lambda i, j: (i, j)
    return pl.pallas_call(
        functools.partial(_qkvg_kernel, d=d, dk=dk, n_q=d // bn),
        grid=(m // bm, nj),
        in_specs=[
            pl.BlockSpec((bm, d), lambda i, j: (i, 0)),
            pl.BlockSpec((1, 1, 3 * d), lambda i, j: (i // tps, 0, 0)),
            pl.BlockSpec((1, d), lambda i, j: (0, 0)),
            pl.BlockSpec((bm, half), lambda i, j: (i % tps, 0)),
            pl.BlockSpec((bm, half), lambda i, j: (i % tps, 0)),
            pl.BlockSpec((None, d, bn), lambda i, j: (r, 0, j)),
            pl.BlockSpec((None, d, bn), lambda i, j: (r, 0, nj + j)),
            pl.BlockSpec((None, d, bn), lambda i, j: (r, 0, 2 * nj + j)),
        ],
        out_specs=[pl.BlockSpec((bm, bn), col), pl.BlockSpec((bm, bn), col), pl.BlockSpec((bm, bn), col)],
        out_shape=[
            jax.ShapeDtypeStruct((m, 2 * d), BF16),
            jax.ShapeDtypeStruct((m, vw), BF16),
            jax.ShapeDtypeStruct((m, vw), F32),
        ],
        scratch_shapes=[pltpu.VMEM((bm, d), BF16), pltpu.VMEM((bm, LANES), F32)],
        compiler_params=_params(("parallel", "arbitrary")),
        name="ret_qkvg_proj",
    )(x, mod, g_pre, cos, sin, w_in, w_in, w_in)


def _retention_kernel(q_ref, k_ref, v_ref, g_ref, gn_ref, dmat_ref, tab_ref, o_ref, state_ref, *, tr):
    tab = tab_ref[0]
    qdec, kdec, cdec = tab[:, 0:1], tab[:, 1:2], tab[0:1, 2:3]
    state_ref[...] = jnp.zeros_like(state_ref)

    for blk in range(q_ref.shape[0] // tr):
        rows = slice(blk * tr, (blk + 1) * tr)
        q = q_ref[rows, :]
        k = k_ref[rows, :]
        v = v_ref[rows, :]
        state = state_ref[...]

        scores = lax.dot_general(q, k, (((1,), (1,)), ((), ())), preferred_element_type=F32)
        p = (scores * dmat_ref[0]).astype(BF16)
        o = jnp.dot(p, v, preferred_element_type=F32)
        o = o + qdec * jnp.dot(q, state.astype(BF16), preferred_element_type=F32)

        kd = (k.astype(F32) * kdec).astype(BF16)
        state_ref[...] = state * cdec + lax.dot_general(
            kd, v, (((0,), (0,)), ((), ())), preferred_element_type=F32)

        mu = jnp.mean(o, axis=-1, keepdims=True)
        oc = o - mu
        var = jnp.mean(oc * oc, axis=-1, keepdims=True)
        on = oc * lax.rsqrt(var + EPS) * gn_ref[...]
        o_ref[rows, :] = (_silu(g_ref[rows, :]) * on).astype(BF16)


def _retention_core(qk, v, g, gn_g, dmat, tab, b, s, t):
    m = qk.shape[0]
    h = RET_HEADS
    tr = t["tr"]
    dk = qk.shape[1] // (2 * h)
    dv = v.shape[1] // h
    return pl.pallas_call(
        functools.partial(_retention_kernel, tr=tr),
        grid=(b, h),
        in_specs=[
            pl.BlockSpec((s, dk), lambda bi, hi: (bi, hi)),
            pl.BlockSpec((s, dk), lambda bi, hi: (bi, h + hi)),
            pl.BlockSpec((s, dv), lambda bi, hi: (bi, hi)),
            pl.BlockSpec((s, dv), lambda bi, hi: (bi, hi)),
            pl.BlockSpec((1, dv), lambda bi, hi: (0, hi)),
            pl.BlockSpec((1, tr, tr), lambda bi, hi: (hi, 0, 0)),
            pl.BlockSpec((1, tr, 128), lambda bi, hi: (hi, 0, 0)),
        ],
        out_specs=pl.BlockSpec((s, dv), lambda bi, hi: (bi, hi)),
        out_shape=jax.ShapeDtypeStruct((m, h * dv), BF16),
        scratch_shapes=[pltpu.VMEM((dk, dv), F32)],
        compiler_params=_params(("parallel", "parallel")),
        name="ret_core",
    )(qk, qk, v, g, gn_g, dmat, tab)


def _retention_tables(s, dk, tr):
    half = dk // 2
    inv = ROPE_BASE ** (-jnp.arange(half, dtype=F32) / half)
    ang = jnp.arange(s, dtype=F32)[:, None] * inv[None, :]
    gamma = 1.0 - jnp.exp2(-5.0 - jnp.arange(RET_HEADS, dtype=F32))
    log_g = jnp.log(gamma)
    r = jnp.arange(tr, dtype=F32)
    n, mm = r[:, None], r[None, :]
    cn, cm = jnp.floor(n / CHUNK), jnp.floor(mm / CHUNK)
    dmat = jnp.where(cm <= cn, jnp.exp(log_g[:, None, None] * jnp.abs(n - mm)), 0.0)
    tab = jnp.zeros((RET_HEADS, tr, 128), F32)
    tab = tab.at[:, :, 0].set(jnp.exp(log_g[:, None] * (r[None, :] + 1.0)))
    tab = tab.at[:, :, 1].set(jnp.exp(log_g[:, None] * (tr - 1.0 - r[None, :])))
    tab = tab.at[:, :, 2].set(jnp.exp(log_g * tr)[:, None])
    return jnp.cos(ang), jnp.sin(ang), dmat, tab


def _outproj_kernel(y_ref, x_ref, mod_ref, gpost_ref, w_ref, o_ref, inv_ref, *, res_w, d):
    k = pl.program_id(1)

    @pl.when(k == 0)
    def _():
        o_ref[...] = jnp.zeros_like(o_ref)

    o_ref[...] += jnp.dot(y_ref[...], w_ref[...], preferred_element_type=F32)

    @pl.when(k == pl.num_programs(1) - 1)
    def _():
        _post_residual_rows(o_ref, x_ref, mod_ref, gpost_ref, o_ref, inv_ref, res_w, d)


def _outproj_sublayer(y, x, mod, g_post, w, r, res_w, s, t):
    m, d = x.shape
    kk = y.shape[1]
    bm, bk = t["bm"], t["bk"]
    tps = s // bm
    return pl.pallas_call(
        functools.partial(_outproj_kernel, res_w=res_w, d=d),
        grid=(m // bm, kk // bk),
        in_specs=[
            pl.BlockSpec((bm, bk), lambda i, k: (i, k)),
            pl.BlockSpec((bm, d), lambda i, k: (i, 0)),
            pl.BlockSpec((1, 1, 3 * d), lambda i, k: (i // tps, 0, 0)),
            pl.BlockSpec((1, d), lambda i, k: (0, 0)),
            pl.BlockSpec((None, bk, d), lambda i, k: (r, k, 0)),
        ],
        out_specs=pl.BlockSpec((bm, d), lambda i, k: (i, 0)),
        out_shape=jax.ShapeDtypeStruct((m, d), F32),
        scratch_shapes=[pltpu.VMEM((bm, LANES), F32)],
        compiler_params=_params(("parallel", "arbitrary")),
        name="ret_outproj",
    )(y, x, mod, g_post, w)


def _glu_kernel(x_ref, mod_ref, gpre_ref, wa_ref, wb_ref, ba_ref, bb_ref, u_ref, h_ref, inv_ref, *, d):
    @pl.when(pl.program_id(1) == 0)
    def _():
        _premod_rows(x_ref, mod_ref, gpre_ref, h_ref, inv_ref, d)

    h = h_ref[...]
    a = jnp.dot(h, wa_ref[...], preferred_element_type=F32) + ba_ref[...]
    b = jnp.dot(h, wb_ref[...], preferred_element_type=F32) + bb_ref[...]
    u_ref[...] = a * jax.nn.sigmoid(b)


def _glu_proj(x, mod, g_pre, w_in, b_in, r, s, t):
    m, d = x.shape
    bm, bn = t["bm"], t["bn"]
    tps = s // bm
    nh = d // bn
    return pl.pallas_call(
        functools.partial(_glu_kernel, d=d),
        grid=(m // bm, nh),
        in_specs=[
            pl.BlockSpec((bm, d), lambda i, j: (i, 0)),
            pl.BlockSpec((1, 1, 3 * d), lambda i, j: (i // tps, 0, 0)),
            pl.BlockSpec((1, d), lambda i, j: (0, 0)),
            pl.BlockSpec((None, d, bn), lambda i, j: (r, 0, j)),
            pl.BlockSpec((None, d, bn), lambda i, j: (r, 0, nh + j)),
            pl.BlockSpec((1, bn), lambda i, j: (0, j)),
            pl.BlockSpec((1, bn), lambda i, j: (0, nh + j)),
        ],
        out_specs=pl.BlockSpec((bm, bn), lambda i, j: (i, j)),
        out_shape=jax.ShapeDtypeStruct((m, d), F32),
        scratch_shapes=[pltpu.VMEM((bm, d), BF16), pltpu.VMEM((bm, LANES), F32)],
        compiler_params=_params(("parallel", "arbitrary")),
        name="conv_glu_proj",
    )(x, mod, g_pre, w_in, w_in, b_in, b_in)


def _conv_kernel(ucur_ref, uprev_ref, x_ref, mod_ref, dw_ref, dwb_ref, lng_ref, lnb_ref, w_ref,
                 bout_ref, gpost_ref, o_ref, ubuf_ref, *, res_w, d, tps, rc, cc):
    bm = ucur_ref.shape[0]
    first = (pl.program_id(0) % tps) == 0
    ubuf_ref[0:CONV_HALO, :] = jnp.where(first, 0.0, uprev_ref[...])
    ubuf_ref[CONV_HALO:CONV_HALO + bm, :] = ucur_ref[...]
    off = CONV_HALO - (CONV_WIDTH - 1)

    def row_body(r, carry):
        base = pl.multiple_of(r * rc, rc)
        for c in range(d // cc):
            cols = slice(c * cc, (c + 1) * cc)
            win = ubuf_ref[pl.ds(base, rc + CONV_HALO), cols]
            acc = jnp.broadcast_to(dwb_ref[:, cols], (rc, cc))
            for phase in range(8):
                part = None
                for tap in range(CONV_WIDTH):
                    o = off + tap
                    if o % 8 == phase:
                        term = win[o:o + rc] * dw_ref[tap:tap + 1, cols]
                        part = term if part is None else part + term
                acc = acc + part
            o_ref[pl.ds(base, rc), cols] = acc
        return carry

    lax.fori_loop(0, bm // rc, row_body, 0)

    cv = o_ref[...]
    mu = jnp.mean(cv, axis=-1, keepdims=True)
    cz = cv - mu
    var = jnp.mean(cz * cz, axis=-1, keepdims=True)
    yn = cz * lax.rsqrt(var + EPS) * lng_ref[...] + lnb_ref[...]
    z = jnp.dot(_silu(yn).astype(BF16), w_ref[...], preferred_element_type=F32) + bout_ref[...]
    gain = (res_w * mod_ref[0, :, 2 * d:3 * d]) * gpost_ref[...]
    inv = lax.rsqrt(jnp.mean(z * z, axis=-1, keepdims=True) + EPS)
    o_ref[...] = x_ref[...] + z * inv * gain


def _conv_sublayer(u, x, mod, dw, dw_b, ln_g, ln_b, w_out, b_out, g_post, r, res_w, s, t):
    m, d = x.shape
    bm = t["bm"]
    tps = s // bm
    hb = bm // CONV_HALO
    rc, cc = min(CONV_ROW_CHUNK, bm), min(CONV_LANE_CHUNK, d)
    full = lambda i: (0, 0)
    return pl.pallas_call(
        functools.partial(_conv_kernel, res_w=res_w, d=d, tps=tps, rc=rc, cc=cc),
        grid=(m // bm,),
        in_specs=[
            pl.BlockSpec((bm, d), lambda i: (i, 0)),
            pl.BlockSpec((CONV_HALO, d), lambda i: (jnp.maximum(i * hb - 1, 0), 0)),
            pl.BlockSpec((bm, d), lambda i: (i, 0)),
            pl.BlockSpec((1, 1, 3 * d), lambda i: (i // tps, 0, 0)),
            pl.BlockSpec(dw.shape, full),
            pl.BlockSpec((1, d), full),
            pl.BlockSpec((1, d), full),
            pl.BlockSpec((1, d), full),
            pl.BlockSpec((None, d, d), lambda i: (r, 0, 0)),
            pl.BlockSpec((1, d), full),
            pl.BlockSpec((1, d), full),
        ],
        out_specs=pl.BlockSpec((bm, d), lambda i: (i, 0)),
        out_shape=jax.ShapeDtypeStruct((m, d), F32),
        scratch_shapes=[pltpu.VMEM((CONV_HALO + bm, d), F32)],
        compiler_params=_params(("parallel",)),
        name="conv_sublayer",
    )(u, u, x, mod, dw, dw_b, ln_g, ln_b, w_out, b_out, g_post)


def kernel(x, c, ada_w, ada_b, norm_pre, norm_post, ffn_w_gate, ffn_w_up, ffn_w_down, ret_w_in, ret_gn,
           ret_w_out, conv_w_in, conv_b_in, conv_dw, conv_dw_b, conv_ln_g, conv_ln_b, conv_w_out, conv_b_out):
    b, s, d = x.shape
    depth = ada_w.shape[0]
    m = b * s
    f = ffn_w_gate.shape[-1]
    t = _tiles(m, s, d, f)
    assert s % t["bm"] == 0 and s % t["tr"] == 0 and t["tr"] % CHUNK == 0 and t["bm"] % CONV_HALO == 0
    assert CONV_HALO >= CONV_WIDTH - 1 and t["bn"] % t["dk"] == 0

    mods = _ada_mods(c, ada_w.reshape(depth * N_SUBLAYERS, d, 3 * d),
                     ada_b.reshape(depth * N_SUBLAYERS, 1, 3 * d), t)
    mods = mods.reshape(depth, N_SUBLAYERS, b, 1, 3 * d)
    row = lambda p: p.reshape(1, -1)
    cos, sin, dmat, tab = _retention_tables(s, t["dk"], t["tr"])
    dw_pad = CONV_HALO - CONV_WIDTH
    wg, wu, wd = ffn_w_gate.astype(BF16), ffn_w_up.astype(BF16), ffn_w_down.astype(BF16)
    ret_in, ret_out = ret_w_in.astype(BF16), ret_w_out.astype(BF16)
    conv_in, conv_out = conv_w_in.astype(BF16), conv_w_out.astype(BF16)
    ffn_tiles = [(t["bm"], t["bf"]), (t["bm"], 2 * t["bf"]), (2 * t["bm"], t["bf"] // 2), (2 * t["bm"], t["bf"])]

    xf = x.reshape(m, d)
    for i in range(depth):
        r = i // 2
        ffn = lambda xin, j, w: _ffn_sublayer(
            xin, mods[i, j], row(norm_pre[i, j]), row(norm_post[i, j]), wg, wu, wd, i, w, 0.5, s,
            *ffn_tiles[(2 * i + w) % len(ffn_tiles)])
        xf = ffn(xf, 0, 0)
        if i % 2 == 0:
            qk, v, g = _qkvg_proj(xf, mods[i, 1], row(norm_pre[i, 1]), cos, sin, ret_in, r, s, t)
            y = _retention_core(qk, v, g, row(ret_gn[r]), dmat, tab, b, s, t)
            xf = _outproj_sublayer(y, xf, mods[i, 1], row(norm_post[i, 1]), ret_out, r, 1.0, s, t)
        else:
            u = _glu_proj(xf, mods[i, 1], row(norm_pre[i, 1]), conv_in, row(conv_b_in[r]), r, s, t)
            xf = _conv_sublayer(u, xf, mods[i, 1], jnp.pad(conv_dw[r], ((0, dw_pad), (0, 0))), row(conv_dw_b[r]),
                                row(conv_ln_g[r]), row(conv_ln_b[r]), conv_out, row(conv_b_out[r]),
                                row(norm_post[i, 1]), r, 1.0, s, t)
        xf = ffn(xf, 2, 1)
    return xf.reshape(b, s, d)
```

```python
import functools

import jax
import jax.numpy as jnp
from jax import lax
from jax.experimental import pallas as pl
from jax.experimental.pallas import tpu as pltpu

EPS = 1e-6
RET_HEADS = 8
CHUNK = 64
CONV_WIDTH = 31
ROPE_BASE = 10000.0
N_SUBLAYERS = 3

F32 = jnp.float32
BF16 = jnp.bfloat16

V7X_VMEM_LIMIT_BYTES = 60 * 1024 * 1024
LANES = 128
NORM_ROW_CHUNK = 16
CONV_HALO = 32
CONV_ROW_CHUNK = 64
CONV_LANE_CHUNK = 256


def _tiles(m, s, d, f):
    dk = d // RET_HEADS
    return dict(
        bm=min(512, s),
        bm_qkvg=min(1024, s),
        bf=min(1024, f),
        bn=min(512, d),
        tr=min(256, s),
        ada_bn=min(1024, 3 * d),
        dk=dk,
    )


def _params(sem):
    return pltpu.CompilerParams(dimension_semantics=sem, vmem_limit_bytes=V7X_VMEM_LIMIT_BYTES)


def _row_loop(n_rows, body, unroll):
    rc = min(NORM_ROW_CHUNK, n_rows)
    n = n_rows // rc

    def step(r, carry):
        body(pl.ds(pl.multiple_of(r * rc, rc), rc))
        return carry

    lax.fori_loop(0, n, step, 0, unroll=min(unroll, n))


def _inv_rms_rows(z_ref, inv_ref, bias_ref=None):
    def body(rows):
        zv = z_ref[rows, :]
        if bias_ref is not None:
            zv = zv + bias_ref[...]
        inv = lax.rsqrt(jnp.mean(zv * zv, axis=-1, keepdims=True) + EPS)
        inv_ref[rows, :] = jnp.broadcast_to(inv, (zv.shape[0], inv_ref.shape[1]))

    _row_loop(z_ref.shape[0], body, unroll=8)


def _premod_rows(x_ref, mod_ref, gpre_ref, h_ref, inv_ref, d):
    _inv_rms_rows(x_ref, inv_ref)
    shift = mod_ref[0, :, 0:d]
    gain = gpre_ref[...] * (1.0 + mod_ref[0, :, d:2 * d])
    lanes = inv_ref.shape[1]

    def body(rows):
        inv = inv_ref[rows, :]
        for j in range(d // lanes):
            cs = slice(j * lanes, (j + 1) * lanes)
            h_ref[rows, cs] = (x_ref[rows, cs] * inv * gain[:, cs] + shift[:, cs]).astype(h_ref.dtype)

    _row_loop(x_ref.shape[0], body, unroll=4)


def _post_residual_rows(z_ref, x_ref, mod_ref, gpost_ref, o_ref, inv_ref, res_w, d, bias_ref=None):
    _inv_rms_rows(z_ref, inv_ref, bias_ref)
    gain = (res_w * mod_ref[0, :, 2 * d:3 * d]) * gpost_ref[...]
    lanes = inv_ref.shape[1]

    def body(rows):
        inv = inv_ref[rows, :]
        for j in range(d // lanes):
            cs = slice(j * lanes, (j + 1) * lanes)
            zv = z_ref[rows, cs]
            if bias_ref is not None:
                zv = zv + bias_ref[:, cs]
            o_ref[rows, cs] = x_ref[rows, cs] + zv * inv * gain[:, cs]

    _row_loop(x_ref.shape[0], body, unroll=4)


def _silu(v):
    return v * jax.nn.sigmoid(v)


def _ada_kernel(c_ref, w_ref, b_ref, o_ref):
    cs = _silu(c_ref[...]).astype(BF16)
    o_ref[0] = jnp.dot(cs, w_ref[0].astype(BF16), preferred_element_type=F32) + b_ref[0]


def _ada_mods(c, ada_w, ada_b, t):
    n, d, e = ada_w.shape
    b = c.shape[0]
    bn = t["ada_bn"]
    return pl.pallas_call(
        _ada_kernel,
        grid=(n, e // bn),
        in_specs=[
            pl.BlockSpec((b, d), lambda i, j: (0, 0)),
            pl.BlockSpec((1, d, bn), lambda i, j: (i, 0, j)),
            pl.BlockSpec((1, 1, bn), lambda i, j: (i, 0, j)),
        ],
        out_specs=pl.BlockSpec((1, b, bn), lambda i, j: (i, 0, j)),
        out_shape=jax.ShapeDtypeStruct((n, b, e), F32),
        compiler_params=_params(("arbitrary", "arbitrary")),
        name="ada_mods",
    )(c, ada_w, ada_b)


def _ffn_kernel(x_ref, mod_ref, gpre_ref, gpost_ref, wg_ref, wu_ref, wd_ref, o_ref, h_ref, inv_ref,
                *, res_w, d):
    f = pl.program_id(1)

    @pl.when(f == 0)
    def _():
        _premod_rows(x_ref, mod_ref, gpre_ref, h_ref, inv_ref, d)
        o_ref[...] = jnp.zeros_like(o_ref)

    h = h_ref[...]
    hg = jnp.dot(h, wg_ref[...], preferred_element_type=F32)
    hu = jnp.dot(h, wu_ref[...], preferred_element_type=F32)
    a = (_silu(hg) * hu).astype(BF16)
    o_ref[...] += jnp.dot(a, wd_ref[...], preferred_element_type=F32)

    @pl.when(f == pl.num_programs(1) - 1)
    def _():
        _post_residual_rows(o_ref, x_ref, mod_ref, gpost_ref, o_ref, inv_ref, res_w, d)


def _ffn_sublayer(x, mod, g_pre, g_post, wg, wu, wd, layer, which, res_w, s, bm, bf):
    m, d = x.shape
    f = wg.shape[-1]
    tps = s // bm
    return pl.pallas_call(
        functools.partial(_ffn_kernel, res_w=res_w, d=d),
        grid=(m // bm, f // bf),
        in_specs=[
            pl.BlockSpec((bm, d), lambda i, j: (i, 0)),
            pl.BlockSpec((1, 1, 3 * d), lambda i, j: (i // tps, 0, 0)),
            pl.BlockSpec((1, d), lambda i, j: (0, 0)),
            pl.BlockSpec((1, d), lambda i, j: (0, 0)),
            pl.BlockSpec((None, None, d, bf), lambda i, j: (layer, which, 0, j)),
            pl.BlockSpec((None, None, d, bf), lambda i, j: (layer, which, 0, j)),
            pl.BlockSpec((None, None, bf, d), lambda i, j: (layer, which, j, 0)),
        ],
        out_specs=pl.BlockSpec((bm, d), lambda i, j: (i, 0)),
        out_shape=jax.ShapeDtypeStruct((m, d), F32),
        scratch_shapes=[pltpu.VMEM((bm, d), BF16), pltpu.VMEM((bm, LANES), F32)],
        compiler_params=_params(("parallel", "arbitrary")),
        name="ffn_sublayer",
    )(x, mod, g_pre, g_post, wg, wu, wd)


def _qkvg_kernel(x_ref, mod_ref, gpre_ref, cos_ref, sin_ref, wqk_ref, wv_ref, wg_ref, qk_ref, v_ref, g_ref,
                 h_ref, inv_ref, *, d, dk, n_q):
    j = pl.program_id(1)

    @pl.when(j == 0)
    def _():
        _premod_rows(x_ref, mod_ref, gpre_ref, h_ref, inv_ref, d)

    h = h_ref[...]
    y = jnp.dot(h, wqk_ref[...], preferred_element_type=F32)
    v_ref[...] = jnp.dot(h, wv_ref[...], preferred_element_type=F32).astype(BF16)
    g_ref[...] = jnp.dot(h, wg_ref[...], preferred_element_type=F32)

    half = dk // 2
    sc = jnp.where(j >= n_q, dk ** -0.5, 1.0).astype(F32)
    cos = cos_ref[...] * sc
    sin = sin_ref[...] * sc
    for hh in range(y.shape[1] // dk):
        t1 = y[:, hh * dk:hh * dk + half]
        t2 = y[:, hh * dk + half:(hh + 1) * dk]
        qk_ref[:, hh * dk:hh * dk + half] = (t1 * cos - t2 * sin).astype(BF16)
        qk_ref[:, hh * dk + half:(hh + 1) * dk] = (t1 * sin + t2 * cos).astype(BF16)


def _qkvg_proj(x, mod, g_pre, cos, sin, w_in, r, s, t):
    m, d = x.shape
    bm, bn, dk = t["bm_qkvg"], t["bn"], t["dk"]
    tps = s // bm
    vw = 2 * d
    nj = vw // bn
    half = dk // 2
    col = lambda i, j: (i, j)
    return pl.pallas_call(
        functools.partial(_qkvg_kernel, d=d, dk=dk, n_q=d // bn),
        grid=(m // bm, nj),
        in_specs=[
            pl.BlockSpec((bm, d), lambda i, j: (i, 0)),
            pl.BlockSpec((1, 1, 3 * d), lambda i, j: (i // tps, 0, 0)),
            pl.BlockSpec((1, d), lambda i, j: (0, 0)),
            pl.BlockSpec((bm, half), lambda i, j: (i % tps, 0)),
            pl.BlockSpec((bm, half), lambda i, j: (i % tps, 0)),
            pl.BlockSpec((None, d, bn), lambda i, j: (r, 0, j)),
            pl.BlockSpec((None, d, bn), lambda i, j: (r, 0, nj + j)),
            pl.BlockSpec((None, d, bn), lambda i, j: (r, 0, 2 * nj + j)),
        ],
        out_specs=[pl.BlockSpec((bm, bn), col), pl.BlockSpec((bm, bn), col), pl.BlockSpec((bm, bn), col)],
        out_shape=[
            jax.ShapeDtypeStruct((m, 2 * d), BF16),
            jax.ShapeDtypeStruct((m, vw), BF16),
            jax.ShapeDtypeStruct((m, vw), F32),
        ],
        scratch_shapes=[pltpu.VMEM((bm, d), BF16), pltpu.VMEM((bm, LANES), F32)],
        compiler_params=_params(("parallel", "arbitrary")),
        name="ret_qkvg_proj",
    )(x, mod, g_pre, cos, sin, w_in, w_in, w_in)


def _retention_kernel(q_ref, k_ref, v_ref, g_ref, gn_ref, dmat_ref, tab_ref, o_ref, state_ref, *, tr):
    tab = tab_ref[0]
    qdec, kdec, cdec = tab[:, 0:1], tab[:, 1:2], tab[0:1, 2:3]
    state_ref[...] = jnp.zeros_like(state_ref)

    for blk in range(q_ref.shape[0] // tr):
        rows = slice(blk * tr, (blk + 1) * tr)
        q = q_ref[rows, :]
        k = k_ref[rows, :]
        v = v_ref[rows, :]
        state = state_ref[...]

        scores = lax.dot_general(q, k, (((1,), (1,)), ((), ())), preferred_element_type=F32)
        p = (scores * dmat_ref[0]).astype(BF16)
        o = jnp.dot(p, v, preferred_element_type=F32)
        o = o + qdec * jnp.dot(q, state.astype(BF16), preferred_element_type=F32)

        kd = (k.astype(F32) * kdec).astype(BF16)
        state_ref[...] = state * cdec + lax.dot_general(
            kd, v, (((0,), (0,)), ((), ())), preferred_element_type=F32)

        mu = jnp.mean(o, axis=-1, keepdims=True)
        oc = o - mu
        var = jnp.mean(oc * oc, axis=-1, keepdims=True)
        on = oc * lax.rsqrt(var + EPS) * gn_ref[...]
        o_ref[rows, :] = (_silu(g_ref[rows, :]) * on).astype(BF16)


def _retention_core(qk, v, g, gn_g, dmat, tab, b, s, t):
    m = qk.shape[0]
    h = RET_HEADS
    tr = t["tr"]
    dk = qk.shape[1] // (2 * h)
    dv = v.shape[1] // h
    return pl.pallas_call(
        functools.partial(_retention_kernel, tr=tr),
        grid=(b, h),
        in_specs=[
            pl.BlockSpec((s, dk), lambda bi, hi: (bi, hi)),
            pl.BlockSpec((s, dk), lambda bi, hi: (bi, h + hi)),
            pl.BlockSpec((s, dv), lambda bi, hi: (bi, hi)),
            pl.BlockSpec((s, dv), lambda bi, hi: (bi, hi)),
            pl.BlockSpec((1, dv), lambda bi, hi: (0, hi)),
            pl.BlockSpec((1, tr, tr), lambda bi, hi: (hi, 0, 0)),
            pl.BlockSpec((1, tr, 128), lambda bi, hi: (hi, 0, 0)),
        ],
        out_specs=pl.BlockSpec((s, dv), lambda bi, hi: (bi, hi)),
        out_shape=jax.ShapeDtypeStruct((m, h * dv), BF16),
        scratch_shapes=[pltpu.VMEM((dk, dv), F32)],
        compiler_params=_params(("parallel", "parallel")),
        name="ret_core",
    )(qk, qk, v, g, gn_g, dmat, tab)


def _retention_tables(s, dk, tr):
    half = dk // 2
    inv = ROPE_BASE ** (-jnp.arange(half, dtype=F32) / half)
    ang = jnp.arange(s, dtype=F32)[:, None] * inv[None, :]
    gamma = 1.0 - jnp.exp2(-5.0 - jnp.arange(RET_HEADS, dtype=F32))
    log_g = jnp.log(gamma)
    r = jnp.arange(tr, dtype=F32)
    n, mm = r[:, None], r[None, :]
    cn, cm = jnp.floor(n / CHUNK), jnp.floor(mm / CHUNK)
    dmat = jnp.where(cm <= cn, jnp.exp(log_g[:, None, None] * jnp.abs(n - mm)), 0.0)
    tab = jnp.zeros((RET_HEADS, tr, 128), F32)
    tab = tab.at[:, :, 0].set(jnp.exp(log_g[:, None] * (r[None, :] + 1.0)))
    tab = tab.at[:, :, 1].set(jnp.exp(log_g[:, None] * (tr - 1.0 - r[None, :])))
    tab = tab.at[:, :, 2].set(jnp.exp(log_g * tr)[:, None])
    return jnp.cos(ang), jnp.sin(ang), dmat, tab


def _outproj_kernel(y_ref, x_ref, mod_ref, gpost_ref, w_ref, o_ref, inv_ref, *, res_w, d):
    o_ref[...] = jnp.dot(y_ref[...], w_ref[...], preferred_element_type=F32)
    _post_residual_rows(o_ref, x_ref, mod_ref, gpost_ref, o_ref, inv_ref, res_w, d)


def _resident(block_shape, index_map):
    return pl.BlockSpec(block_shape, index_map, pipeline_mode=pl.Buffered(1))


def _outproj_sublayer(y, x, mod, g_post, w, r, res_w, s, t):
    m, d = x.shape
    kk = y.shape[1]
    bm = t["bm"]
    tps = s // bm
    return pl.pallas_call(
        functools.partial(_outproj_kernel, res_w=res_w, d=d),
        grid=(m // bm,),
        in_specs=[
            pl.BlockSpec((bm, kk), lambda i: (i, 0)),
            pl.BlockSpec((bm, d), lambda i: (i, 0)),
            pl.BlockSpec((1, 1, 3 * d), lambda i: (i // tps, 0, 0)),
            pl.BlockSpec((1, d), lambda i: (0, 0)),
            _resident((None, kk, d), lambda i: (r, 0, 0)),
        ],
        out_specs=pl.BlockSpec((bm, d), lambda i: (i, 0)),
        out_shape=jax.ShapeDtypeStruct((m, d), F32),
        scratch_shapes=[pltpu.VMEM((bm, LANES), F32)],
        compiler_params=_params(("parallel",)),
        name="ret_outproj",
    )(y, x, mod, g_post, w)


def _glu_kernel(x_ref, mod_ref, gpre_ref, w_ref, b_ref, u_ref, h_ref, inv_ref, *, d, bn):
    _premod_rows(x_ref, mod_ref, gpre_ref, h_ref, inv_ref, d)
    h = h_ref[...]
    for j in range(d // bn):
        ca, cb = slice(j * bn, (j + 1) * bn), slice(d + j * bn, d + (j + 1) * bn)
        a = jnp.dot(h, w_ref[:, ca], preferred_element_type=F32) + b_ref[:, ca]
        b = jnp.dot(h, w_ref[:, cb], preferred_element_type=F32) + b_ref[:, cb]
        u_ref[:, ca] = a * jax.nn.sigmoid(b)


def _glu_proj(x, mod, g_pre, w_in, b_in, r, s, t):
    m, d = x.shape
    bm, bn = t["bm"], t["bn"]
    tps = s // bm
    return pl.pallas_call(
        functools.partial(_glu_kernel, d=d, bn=bn),
        grid=(m // bm,),
        in_specs=[
            pl.BlockSpec((bm, d), lambda i: (i, 0)),
            pl.BlockSpec((1, 1, 3 * d), lambda i: (i // tps, 0, 0)),
            pl.BlockSpec((1, d), lambda i: (0, 0)),
            _resident((None, d, 2 * d), lambda i: (r, 0, 0)),
            pl.BlockSpec((1, 2 * d), lambda i: (0, 0)),
        ],
        out_specs=pl.BlockSpec((bm, d), lambda i: (i, 0)),
        out_shape=jax.ShapeDtypeStruct((m, d), F32),
        scratch_shapes=[pltpu.VMEM((bm, d), BF16), pltpu.VMEM((bm, LANES), F32)],
        compiler_params=_params(("parallel",)),
        name="conv_glu_proj",
    )(x, mod, g_pre, w_in, b_in)


def _conv_kernel(ucur_ref, uprev_ref, x_ref, mod_ref, dw_ref, dwb_ref, lng_ref, lnb_ref, w_ref,
                 bout_ref, gpost_ref, o_ref, ubuf_ref, *, res_w, d, tps, rc, cc):
    bm = ucur_ref.shape[0]
    first = (pl.program_id(0) % tps) == 0
    ubuf_ref[0:CONV_HALO, :] = jnp.where(first, 0.0, uprev_ref[...])
    ubuf_ref[CONV_HALO:CONV_HALO + bm, :] = ucur_ref[...]
    off = CONV_HALO - (CONV_WIDTH - 1)

    def row_body(r, carry):
        base = pl.multiple_of(r * rc, rc)
        for c in range(d // cc):
            cols = slice(c * cc, (c + 1) * cc)
            win = ubuf_ref[pl.ds(base, rc + CONV_HALO), cols]
            acc = jnp.broadcast_to(dwb_ref[:, cols], (rc, cc))
            for phase in range(8):
                part = None
                for tap in range(CONV_WIDTH):
                    o = off + tap
                    if o % 8 == phase:
                        term = win[o:o + rc] * dw_ref[tap:tap + 1, cols]
                        part = term if part is None else part + term
                acc = acc + part
            o_ref[pl.ds(base, rc), cols] = acc
        return carry

    lax.fori_loop(0, bm // rc, row_body, 0)

    cv = o_ref[...]
    mu = jnp.mean(cv, axis=-1, keepdims=True)
    cz = cv - mu
    var = jnp.mean(cz * cz, axis=-1, keepdims=True)
    yn = cz * lax.rsqrt(var + EPS) * lng_ref[...] + lnb_ref[...]
    z = jnp.dot(_silu(yn).astype(BF16), w_ref[...], preferred_element_type=F32) + bout_ref[...]
    gain = (res_w * mod_ref[0, :, 2 * d:3 * d]) * gpost_ref[...]
    inv = lax.rsqrt(jnp.mean(z * z, axis=-1, keepdims=True) + EPS)
    o_ref[...] = x_ref[...] + z * inv * gain


def _conv_sublayer(u, x, mod, dw, dw_b, ln_g, ln_b, w_out, b_out, g_post, r, res_w, s, t):
    m, d = x.shape
    bm = t["bm"]
    tps = s // bm
    hb = bm // CONV_HALO
    rc, cc = min(CONV_ROW_CHUNK, bm), min(CONV_LANE_CHUNK, d)
    full = lambda i: (0, 0)
    return pl.pallas_call(
        functools.partial(_conv_kernel, res_w=res_w, d=d, tps=tps, rc=rc, cc=cc),
        grid=(m // bm,),
        in_specs=[
            pl.BlockSpec((bm, d), lambda i: (i, 0)),
            pl.BlockSpec((CONV_HALO, d), lambda i: (jnp.maximum(i * hb - 1, 0), 0)),
            pl.BlockSpec((bm, d), lambda i: (i, 0)),
            pl.BlockSpec((1, 1, 3 * d), lambda i: (i // tps, 0, 0)),
            pl.BlockSpec(dw.shape, full),
            pl.BlockSpec((1, d), full),
            pl.BlockSpec((1, d), full),
            pl.BlockSpec((1, d), full),
            _resident((None, d, d), lambda i: (r, 0, 0)),
            pl.BlockSpec((1, d), full),
            pl.BlockSpec((1, d), full),
        ],
        out_specs=pl.BlockSpec((bm, d), lambda i: (i, 0)),
        out_shape=jax.ShapeDtypeStruct((m, d), F32),
        scratch_shapes=[pltpu.VMEM((CONV_HALO + bm, d), F32)],
        compiler_params=_params(("parallel",)),
        name="conv_sublayer",
    )(u, u, x, mod, dw, dw_b, ln_g, ln_b, w_out, b_out, g_post)


def kernel(x, c, ada_w, ada_b, norm_pre, norm_post, ffn_w_gate, ffn_w_up, ffn_w_down, ret_w_in, ret_gn,
           ret_w_out, conv_w_in, conv_b_in, conv_dw, conv_dw_b, conv_ln_g, conv_ln_b, conv_w_out, conv_b_out):
    b, s, d = x.shape
    depth = ada_w.shape[0]
    m = b * s
    f = ffn_w_gate.shape[-1]
    t = _tiles(m, s, d, f)
    assert s % t["bm"] == 0 and s % t["tr"] == 0 and t["tr"] % CHUNK == 0 and t["bm"] % CONV_HALO == 0
    assert CONV_HALO >= CONV_WIDTH - 1 and t["bn"] % t["dk"] == 0

    mods = _ada_mods(c, ada_w.reshape(depth * N_SUBLAYERS, d, 3 * d),
                     ada_b.reshape(depth * N_SUBLAYERS, 1, 3 * d), t)
    mods = mods.reshape(depth, N_SUBLAYERS, b, 1, 3 * d)
    row = lambda p: p.reshape(1, -1)
    cos, sin, dmat, tab = _retention_tables(s, t["dk"], t["tr"])
    dw_pad = CONV_HALO - CONV_WIDTH
    wg, wu, wd = ffn_w_gate.astype(BF16), ffn_w_up.astype(BF16), ffn_w_down.astype(BF16)
    ret_in, ret_out = ret_w_in.astype(BF16), ret_w_out.astype(BF16)
    conv_in, conv_out = conv_w_in.astype(BF16), conv_w_out.astype(BF16)

    xf = x.reshape(m, d)
    for i in range(depth):
        r = i // 2
        ffn = lambda xin, j, w: _ffn_sublayer(
            xin, mods[i, j], row(norm_pre[i, j]), row(norm_post[i, j]), wg, wu, wd, i, w, 0.5, s,
            t["bm"], t["bf"])
        xf = ffn(xf, 0, 0)
        if i % 2 == 0:
            qk, v, g = _qkvg_proj(xf, mods[i, 1], row(norm_pre[i, 1]), cos, sin, ret_in, r, s, t)
            y = _retention_core(qk, v, g, row(ret_gn[r]), dmat, tab, b, s, t)
            xf = _outproj_sublayer(y, xf, mods[i, 1], row(norm_post[i, 1]), ret_out, r, 1.0, s, t)
        else:
            u = _glu_proj(xf, mods[i, 1], row(norm_pre[i, 1]), conv_in, row(conv_b_in[r]), r, s, t)
            xf = _conv_sublayer(u, xf, mods[i, 1], jnp.pad(conv_dw[r], ((0, dw_pad), (0, 0))), row(conv_dw_b[r]),
                                row(conv_ln_g[r]), row(conv_ln_b[r]), conv_out, row(conv_b_out[r]),
                                row(norm_post[i, 1]), r, 1.0, s, t)
        xf = ffn(xf, 2, 1)
    return xf.reshape(b, s, d)
```

```python
import functools

import jax
import jax.numpy as jnp
from jax import lax
from jax.experimental import pallas as pl
from jax.experimental.pallas import tpu as pltpu

EPS = 1e-6
RET_HEADS = 8
CHUNK = 64
CONV_WIDTH = 31
ROPE_BASE = 10000.0
N_SUBLAYERS = 3

F32 = jnp.float32
BF16 = jnp.bfloat16

V7X_VMEM_LIMIT_BYTES = 60 * 1024 * 1024
LANES = 128
NORM_ROW_CHUNK = 16
CONV_HALO = 32
CONV_ROW_CHUNK = 64
CONV_LANE_CHUNK = LANES


def _tiles(m, s, d, f):
    dk = d // RET_HEADS
    return dict(
        bm=min(512, s),
        bm_qkvg=min(1024, s),
        bf=min(1024, f),
        bn=min(512, d),
        tr=min(256, s),
        ada_bn=min(1024, 3 * d),
        dk=dk,
    )


def _params(sem):
    return pltpu.CompilerParams(dimension_semantics=sem, vmem_limit_bytes=V7X_VMEM_LIMIT_BYTES)


def _row_loop(n_rows, body, unroll):
    rc = min(NORM_ROW_CHUNK, n_rows)
    n = n_rows // rc

    def step(r, carry):
        body(pl.ds(pl.multiple_of(r * rc, rc), rc))
        return carry

    lax.fori_loop(0, n, step, 0, unroll=min(unroll, n))


def _inv_rms_rows(z_ref, inv_ref, bias_ref=None):
    def body(rows):
        zv = z_ref[rows, :]
        if bias_ref is not None:
            zv = zv + bias_ref[...]
        inv = lax.rsqrt(jnp.mean(zv * zv, axis=-1, keepdims=True) + EPS)
        inv_ref[rows, :] = jnp.broadcast_to(inv, (zv.shape[0], inv_ref.shape[1]))

    _row_loop(z_ref.shape[0], body, unroll=8)


def _premod_rows(x_ref, mod_ref, gpre_ref, h_ref, inv_ref, d):
    _inv_rms_rows(x_ref, inv_ref)
    shift = mod_ref[0, :, 0:d]
    gain = gpre_ref[...] * (1.0 + mod_ref[0, :, d:2 * d])
    lanes = inv_ref.shape[1]

    def body(rows):
        inv = inv_ref[rows, :]
        for j in range(d // lanes):
            cs = slice(j * lanes, (j + 1) * lanes)
            h_ref[rows, cs] = (x_ref[rows, cs] * inv * gain[:, cs] + shift[:, cs]).astype(h_ref.dtype)

    _row_loop(x_ref.shape[0], body, unroll=4)


def _premod_next(xn_ref, modn_ref, gpre_ref, h_ref, slot, row0, d):
    n_rows = xn_ref.shape[0]
    rc = min(NORM_ROW_CHUNK, n_rows)
    shift = modn_ref[0, :, 0:d]
    gain = gpre_ref[...] * (1.0 + modn_ref[0, :, d:2 * d])
    for c in range(n_rows // rc):
        xv = xn_ref[c * rc:(c + 1) * rc, :]
        inv = lax.rsqrt(jnp.mean(xv * xv, axis=-1, keepdims=True) + EPS)
        start = row0 + c * rc
        rows = pl.ds(start if isinstance(start, int) else pl.multiple_of(start, rc), rc)
        h_ref[slot, rows, :] = (xv * inv * gain + shift).astype(h_ref.dtype)


def _post_residual_rows(z_ref, x_ref, mod_ref, gpost_ref, o_ref, inv_ref, res_w, d, bias_ref=None):
    _inv_rms_rows(z_ref, inv_ref, bias_ref)
    gain = (res_w * mod_ref[0, :, 2 * d:3 * d]) * gpost_ref[...]
    lanes = inv_ref.shape[1]

    def body(rows):
        inv = inv_ref[rows, :]
        for j in range(d // lanes):
            cs = slice(j * lanes, (j + 1) * lanes)
            zv = z_ref[rows, cs]
            if bias_ref is not None:
                zv = zv + bias_ref[:, cs]
            o_ref[rows, cs] = x_ref[rows, cs] + zv * inv * gain[:, cs]

    _row_loop(x_ref.shape[0], body, unroll=4)


def _post_residual_slabs(z_ref, x_ref, mod_ref, gpost_ref, o_ref, inv_ref, bias_ref, res_w, d):
    n_slabs, n_rows, zw = z_ref.shape
    lw = min(inv_ref.shape[1], zw)
    gain = (res_w * mod_ref[0, :, 2 * d:3 * d]) * gpost_ref[...]

    def sumsq_body(rows):
        ss = None
        for j in range(n_slabs):
            zv = z_ref[j, rows, :] + bias_ref[:, j * zw:(j + 1) * zw]
            part = jnp.sum(zv * zv, axis=-1, keepdims=True)
            ss = part if ss is None else ss + part
        inv = lax.rsqrt(ss * (1.0 / d) + EPS)
        inv_ref[rows, :] = jnp.broadcast_to(inv, (ss.shape[0], inv_ref.shape[1]))

    _row_loop(n_rows, sumsq_body, unroll=8)

    def out_body(rows):
        inv = inv_ref[rows, 0:lw]
        for j in range(n_slabs):
            for q in range(zw // lw):
                cs = slice(j * zw + q * lw, j * zw + (q + 1) * lw)
                zv = z_ref[j, rows, q * lw:(q + 1) * lw] + bias_ref[:, cs]
                o_ref[rows, cs] = x_ref[rows, cs] + zv * inv * gain[:, cs]

    _row_loop(n_rows, out_body, unroll=4)


def _silu(v):
    return v * jax.nn.sigmoid(v)


def _ada_kernel(c_ref, w_ref, b_ref, o_ref):
    cs = _silu(c_ref[...]).astype(BF16)
    o_ref[0] = jnp.dot(cs, w_ref[0].astype(BF16), preferred_element_type=F32) + b_ref[0]


def _ada_mods(c, ada_w, ada_b, t):
    n, d, e = ada_w.shape
    b = c.shape[0]
    bn = t["ada_bn"]
    return pl.pallas_call(
        _ada_kernel,
        grid=(n, e // bn),
        in_specs=[
            pl.BlockSpec((b, d), lambda i, j: (0, 0)),
            pl.BlockSpec((1, d, bn), lambda i, j: (i, 0, j)),
            pl.BlockSpec((1, 1, bn), lambda i, j: (i, 0, j)),
        ],
        out_specs=pl.BlockSpec((1, b, bn), lambda i, j: (i, 0, j)),
        out_shape=jax.ShapeDtypeStruct((n, b, e), F32),
        compiler_params=_params(("arbitrary", "arbitrary")),
        name="ada_mods",
    )(c, ada_w, ada_b)


def _ffn_kernel(x_ref, xn_ref, mod_ref, modn_ref, gpre_ref, gpost_ref, wg_ref, wu_ref, wd_ref, o_ref,
                h_ref, inv_ref, *, res_w, d):
    i = pl.program_id(0)
    f = pl.program_id(1)
    slot = lax.rem(i, 2)

    @pl.when(jnp.logical_and(i == 0, f == 0))
    def _():
        _premod_rows(x_ref, mod_ref, gpre_ref, h_ref.at[0], inv_ref, d)

    @pl.when(f == 0)
    def _():
        o_ref[...] = jnp.zeros_like(o_ref)

    h = h_ref[slot]
    hg = jnp.dot(h, wg_ref[...], preferred_element_type=F32)
    hu = jnp.dot(h, wu_ref[...], preferred_element_type=F32)
    a = (_silu(hg) * hu).astype(BF16)
    _premod_next(xn_ref, modn_ref, gpre_ref, h_ref, 1 - slot, f * xn_ref.shape[0], d)
    o_ref[...] += jnp.dot(a, wd_ref[...], preferred_element_type=F32)

    @pl.when(f == pl.num_programs(1) - 1)
    def _():
        _post_residual_rows(o_ref, x_ref, mod_ref, gpost_ref, o_ref, inv_ref, res_w, d)


def _ffn_sublayer(x, mod, g_pre, g_post, wg, wu, wd, layer, which, res_w, s, bm, bf):
    m, d = x.shape
    f = wg.shape[-1]
    tps = s // bm
    n_tiles, nf = m // bm, f // bf
    rps = bm // nf
    assert bm % nf == 0 and rps % min(NORM_ROW_CHUNK, rps) == 0
    nxt = lambda i: jnp.minimum(i + 1, n_tiles - 1)
    return pl.pallas_call(
        functools.partial(_ffn_kernel, res_w=res_w, d=d),
        grid=(n_tiles, nf),
        in_specs=[
            pl.BlockSpec((bm, d), lambda i, j: (i, 0)),
            pl.BlockSpec((rps, d), lambda i, j: (nxt(i) * nf + j, 0)),
            pl.BlockSpec((1, 1, 3 * d), lambda i, j: (i // tps, 0, 0)),
            pl.BlockSpec((1, 1, 3 * d), lambda i, j: (nxt(i) // tps, 0, 0)),
            pl.BlockSpec((1, d), lambda i, j: (0, 0)),
            pl.BlockSpec((1, d), lambda i, j: (0, 0)),
            pl.BlockSpec((None, None, d, bf), lambda i, j: (layer, which, 0, j)),
            pl.BlockSpec((None, None, d, bf), lambda i, j: (layer, which, 0, j)),
            pl.BlockSpec((None, None, bf, d), lambda i, j: (layer, which, j, 0)),
        ],
        out_specs=pl.BlockSpec((bm, d), lambda i, j: (i, 0)),
        out_shape=jax.ShapeDtypeStruct((m, d), F32),
        scratch_shapes=[pltpu.VMEM((2, bm, d), BF16), pltpu.VMEM((bm, LANES), F32)],
        compiler_params=_params(("arbitrary", "arbitrary")),
        name="ffn_sublayer",
    )(x, x, mod, mod, g_pre, g_post, wg, wu, wd)


def _qkvg_kernel(x0_ref, xn_ref, mod0_ref, modn_ref, gpre_ref, cos_ref, sin_ref, wqk_ref, wv_ref, wg_ref,
                 qk_ref, v_ref, g_ref, h_ref, inv_ref, *, d, dk, n_q):
    i = pl.program_id(0)
    j = pl.program_id(1)
    slot = lax.rem(i, 2)

    @pl.when(jnp.logical_and(i == 0, j == 0))
    def _():
        _premod_rows(x0_ref, mod0_ref, gpre_ref, h_ref.at[0], inv_ref, d)

    h = h_ref[slot]
    g_ref[...] = _silu(jnp.dot(h, wg_ref[...], preferred_element_type=F32))
    y = jnp.dot(h, wqk_ref[...], preferred_element_type=F32)
    _premod_next(xn_ref, modn_ref, gpre_ref, h_ref, 1 - slot, j * xn_ref.shape[0], d)
    v_ref[...] = jnp.dot(h, wv_ref[...], preferred_element_type=F32).astype(BF16)

    half = dk // 2
    sc = jnp.where(j >= n_q, dk ** -0.5, 1.0).astype(F32)
    cos = cos_ref[...] * sc
    sin = sin_ref[...] * sc
    for hh in range(y.shape[1] // dk):
        t1 = y[:, hh * dk:hh * dk + half]
        t2 = y[:, hh * dk + half:(hh + 1) * dk]
        qk_ref[:, hh * dk:hh * dk + half] = (t1 * cos - t2 * sin).astype(BF16)
        qk_ref[:, hh * dk + half:(hh + 1) * dk] = (t1 * sin + t2 * cos).astype(BF16)


def _qkvg_proj(x, mod, g_pre, cos, sin, w_in, r, s, t):
    m, d = x.shape
    bm, bn, dk = t["bm_qkvg"], t["bn"], t["dk"]
    tps = s // bm
    vw = 2 * d
    nj = vw // bn
    n_tiles = m // bm
    rps = bm // nj
    assert bm % nj == 0 and rps % min(NORM_ROW_CHUNK, rps) == 0
    nxt = lambda i: jnp.minimum(i + 1, n_tiles - 1)
    half = dk // 2
    col = lambda i, j: (i, j)
    return pl.pallas_call(
        functools.partial(_qkvg_kernel, d=d, dk=dk, n_q=d // bn),
        grid=(n_tiles, nj),
        in_specs=[
            _resident((bm, d), lambda i, j: (0, 0)),
            pl.BlockSpec((rps, d), lambda i, j: (nxt(i) * nj + j, 0)),
            _resident((1, 1, 3 * d), lambda i, j: (0, 0, 0)),
            pl.BlockSpec((1, 1, 3 * d), lambda i, j: (nxt(i) // tps, 0, 0)),
            pl.BlockSpec((1, d), lambda i, j: (0, 0)),
            pl.BlockSpec((bm, half), lambda i, j: (i % tps, 0)),
            pl.BlockSpec((bm, half), lambda i, j: (i % tps, 0)),
            pl.BlockSpec((None, d, bn), lambda i, j: (r, 0, j)),
            pl.BlockSpec((None, d, bn), lambda i, j: (r, 0, nj + j)),
            pl.BlockSpec((None, d, bn), lambda i, j: (r, 0, 2 * nj + j)),
        ],
        out_specs=[pl.BlockSpec((bm, bn), col), pl.BlockSpec((bm, bn), col), pl.BlockSpec((bm, bn), col)],
        out_shape=[
            jax.ShapeDtypeStruct((m, 2 * d), BF16),
            jax.ShapeDtypeStruct((m, vw), BF16),
            jax.ShapeDtypeStruct((m, vw), F32),
        ],
        scratch_shapes=[pltpu.VMEM((2, bm, d), BF16), pltpu.VMEM((bm, LANES), F32)],
        compiler_params=_params(("arbitrary", "arbitrary")),
        name="ret_qkvg_proj",
    )(x, x, mod, mod, g_pre, cos, sin, w_in, w_in, w_in)


def _retention_kernel(q_ref, k_ref, v_ref, g_ref, gn_ref, dmat_ref, tab_ref, o_ref, state_ref, *, tr):
    tab = tab_ref[0]
    qdec, kdec, cdec = tab[:, 0:1], tab[:, 1:2], tab[0:1, 2:3]
    state_ref[...] = jnp.zeros_like(state_ref)

    for blk in range(q_ref.shape[0] // tr):
        rows = slice(blk * tr, (blk + 1) * tr)
        q = q_ref[rows, :]
        k = k_ref[rows, :]
        v = v_ref[rows, :]
        state = state_ref[...]

        scores = lax.dot_general(q, k, (((1,), (1,)), ((), ())), preferred_element_type=F32)
        p = (scores * dmat_ref[0]).astype(BF16)
        o = jnp.dot(p, v, preferred_element_type=F32)
        o = o + qdec * jnp.dot(q, state.astype(BF16), preferred_element_type=F32)

        kd = (k.astype(F32) * kdec).astype(BF16)
        state_ref[...] = state * cdec + lax.dot_general(
            kd, v, (((0,), (0,)), ((), ())), preferred_element_type=F32)

        mu = jnp.mean(o, axis=-1, keepdims=True)
        oc = o - mu
        var = jnp.mean(oc * oc, axis=-1, keepdims=True)
        on = oc * lax.rsqrt(var + EPS) * gn_ref[...]
        o_ref[rows, :] = (g_ref[rows, :] * on).astype(BF16)


def _retention_core(qk, v, g, gn_g, dmat, tab, b, s, t):
    m = qk.shape[0]
    h = RET_HEADS
    tr = t["tr"]
    dk = qk.shape[1] // (2 * h)
    dv = v.shape[1] // h
    return pl.pallas_call(
        functools.partial(_retention_kernel, tr=tr),
        grid=(b, h),
        in_specs=[
            pl.BlockSpec((s, dk), lambda bi, hi: (bi, hi)),
            pl.BlockSpec((s, dk), lambda bi, hi: (bi, h + hi)),
            pl.BlockSpec((s, dv), lambda bi, hi: (bi, hi)),
            pl.BlockSpec((s, dv), lambda bi, hi: (bi, hi)),
            pl.BlockSpec((1, dv), lambda bi, hi: (0, hi)),
            pl.BlockSpec((1, tr, tr), lambda bi, hi: (hi, 0, 0)),
            pl.BlockSpec((1, tr, 128), lambda bi, hi: (hi, 0, 0)),
        ],
        out_specs=pl.BlockSpec((s, dv), lambda bi, hi: (bi, hi)),
        out_shape=jax.ShapeDtypeStruct((m, h * dv), BF16),
        scratch_shapes=[pltpu.VMEM((dk, dv), F32)],
        compiler_params=_params(("parallel", "parallel")),
        name="ret_core",
    )(qk, qk, v, g, gn_g, dmat, tab)


def _retention_tables(s, dk, tr):
    half = dk // 2
    inv = ROPE_BASE ** (-jnp.arange(half, dtype=F32) / half)
    ang = jnp.arange(s, dtype=F32)[:, None] * inv[None, :]
    gamma = 1.0 - jnp.exp2(-5.0 - jnp.arange(RET_HEADS, dtype=F32))
    log_g = jnp.log(gamma)
    r = jnp.arange(tr, dtype=F32)
    n, mm = r[:, None], r[None, :]
    cn, cm = jnp.floor(n / CHUNK), jnp.floor(mm / CHUNK)
    dmat = jnp.where(cm <= cn, jnp.exp(log_g[:, None, None] * jnp.abs(n - mm)), 0.0)
    tab = jnp.zeros((RET_HEADS, tr, 128), F32)
    tab = tab.at[:, :, 0].set(jnp.exp(log_g[:, None] * (r[None, :] + 1.0)))
    tab = tab.at[:, :, 1].set(jnp.exp(log_g[:, None] * (tr - 1.0 - r[None, :])))
    tab = tab.at[:, :, 2].set(jnp.exp(log_g * tr)[:, None])
    return jnp.cos(ang), jnp.sin(ang), dmat, tab


def _outproj_kernel(y_ref, x_ref, mod_ref, gpost_ref, w_ref, o_ref, inv_ref, *, res_w, d):
    o_ref[...] = jnp.dot(y_ref[...], w_ref[...], preferred_element_type=F32)
    _post_residual_rows(o_ref, x_ref, mod_ref, gpost_ref, o_ref, inv_ref, res_w, d)


def _resident(block_shape, index_map):
    return pl.BlockSpec(block_shape, index_map, pipeline_mode=pl.Buffered(1))


def _outproj_sublayer(y, x, mod, g_post, w, r, res_w, s, t):
    m, d = x.shape
    kk = y.shape[1]
    bm = t["bm"]
    tps = s // bm
    return pl.pallas_call(
        functools.partial(_outproj_kernel, res_w=res_w, d=d),
        grid=(m // bm,),
        in_specs=[
            pl.BlockSpec((bm, kk), lambda i: (i, 0)),
            pl.BlockSpec((bm, d), lambda i: (i, 0)),
            pl.BlockSpec((1, 1, 3 * d), lambda i: (i // tps, 0, 0)),
            pl.BlockSpec((1, d), lambda i: (0, 0)),
            _resident((None, kk, d), lambda i: (r, 0, 0)),
        ],
        out_specs=pl.BlockSpec((bm, d), lambda i: (i, 0)),
        out_shape=jax.ShapeDtypeStruct((m, d), F32),
        scratch_shapes=[pltpu.VMEM((bm, LANES), F32)],
        compiler_params=_params(("parallel",)),
        name="ret_outproj",
    )(y, x, mod, g_post, w)


def _glu_kernel(x0_ref, xn_ref, mod0_ref, modn_ref, gpre_ref, w_ref, b_ref, u_ref, h_ref, inv_ref, *, d, bn):
    i = pl.program_id(0)
    slot = lax.rem(i, 2)

    @pl.when(i == 0)
    def _():
        _premod_rows(x0_ref, mod0_ref, gpre_ref, h_ref.at[0], inv_ref, d)

    h = h_ref[slot]
    for j in range(d // bn):
        ca, cb = slice(j * bn, (j + 1) * bn), slice(d + j * bn, d + (j + 1) * bn)
        a = jnp.dot(h, w_ref[:, ca], preferred_element_type=F32) + b_ref[:, ca]
        b = jnp.dot(h, w_ref[:, cb], preferred_element_type=F32) + b_ref[:, cb]
        u_ref[:, ca] = a * jax.nn.sigmoid(b)

    _premod_next(xn_ref, modn_ref, gpre_ref, h_ref, 1 - slot, 0, d)


def _glu_proj(x, mod, g_pre, w_in, b_in, r, s, t):
    m, d = x.shape
    bm, bn = t["bm"], t["bn"]
    tps = s // bm
    n_tiles = m // bm
    nxt = lambda i: jnp.minimum(i + 1, n_tiles - 1)
    return pl.pallas_call(
        functools.partial(_glu_kernel, d=d, bn=bn),
        grid=(n_tiles,),
        in_specs=[
            _resident((bm, d), lambda i: (0, 0)),
            pl.BlockSpec((bm, d), lambda i: (nxt(i), 0)),
            _resident((1, 1, 3 * d), lambda i: (0, 0, 0)),
            pl.BlockSpec((1, 1, 3 * d), lambda i: (nxt(i) // tps, 0, 0)),
            pl.BlockSpec((1, d), lambda i: (0, 0)),
            _resident((None, d, 2 * d), lambda i: (r, 0, 0)),
            pl.BlockSpec((1, 2 * d), lambda i: (0, 0)),
        ],
        out_specs=pl.BlockSpec((bm, d), lambda i: (i, 0)),
        out_shape=jax.ShapeDtypeStruct((m, d), F32),
        scratch_shapes=[pltpu.VMEM((2, bm, d), BF16), pltpu.VMEM((bm, LANES), F32)],
        compiler_params=_params(("arbitrary",)),
        name="conv_glu_proj",
    )(x, x, mod, mod, g_pre, w_in, b_in)


def _conv_kernel(ucur_ref, uprev_ref, x_ref, mod_ref, dw_ref, dwb_ref, lng_ref, lnb_ref, w_ref,
                 bout_ref, gpost_ref, o_ref, ubuf_ref, cbuf_ref, hc_ref, z_ref, inv_ref,
                 *, res_w, d, tps, n_tiles, rc, cc):
    i = pl.program_id(0)
    bm = ucur_ref.shape[0]

    @pl.when(i == 0)
    def _():
        hc_ref[...] = jnp.zeros_like(hc_ref)

    first = (jnp.minimum(i, n_tiles - 1) % tps) == 0
    for c in range(d // cc):
        cols = slice(c * cc, (c + 1) * cc)
        ubuf_ref[c, 0:CONV_HALO, :] = jnp.where(first, 0.0, uprev_ref[:, cols])
        ubuf_ref[c, CONV_HALO:CONV_HALO + bm, :] = ucur_ref[:, cols]
    off = CONV_HALO - (CONV_WIDTH - 1)

    def blk_body(blk, carry):
        base = pl.multiple_of(blk * rc, rc)
        for c in range(d // cc):
            cols = slice(c * cc, (c + 1) * cc)
            acc = jnp.broadcast_to(dwb_ref[:, cols], (rc, cc))
            for tap in range(CONV_WIDTH):
                acc = acc + ubuf_ref[c, pl.ds(base + off + tap, rc), :] * dw_ref[tap:tap + 1, cols]
            cbuf_ref[pl.ds(base, rc), cols] = acc
        z_ref[blk] = jnp.dot(hc_ref[...], w_ref[blk], preferred_element_type=F32)
        return carry

    lax.fori_loop(0, bm // rc, blk_body, 0)

    @pl.when(i > 0)
    def _():
        _post_residual_slabs(z_ref, x_ref, mod_ref, gpost_ref, o_ref, inv_ref, bout_ref, res_w, d)

    cv = cbuf_ref[...]
    mu = jnp.mean(cv, axis=-1, keepdims=True)
    cz = cv - mu
    var = jnp.mean(cz * cz, axis=-1, keepdims=True)
    yn = cz * lax.rsqrt(var + EPS) * lng_ref[...] + lnb_ref[...]
    hc_ref[...] = _silu(yn).astype(BF16)


def _conv_sublayer(u, x, mod, dw, dw_b, ln_g, ln_b, w_out, b_out, g_post, r, res_w, s, t):
    m, d = x.shape
    bm = t["bm"]
    tps = s // bm
    n_tiles = m // bm
    hb = bm // CONV_HALO
    rc, cc = min(CONV_ROW_CHUNK, bm), min(CONV_LANE_CHUNK, d)
    n_blk = bm // rc
    assert w_out.shape[1:] == (n_blk, d, d // n_blk)
    full = lambda i: (0, 0)
    cur = lambda i: jnp.minimum(i, n_tiles - 1)
    prev = lambda i: jnp.maximum(i - 1, 0)
    return pl.pallas_call(
        functools.partial(_conv_kernel, res_w=res_w, d=d, tps=tps, n_tiles=n_tiles, rc=rc, cc=cc),
        grid=(n_tiles + 1,),
        in_specs=[
            pl.BlockSpec((bm, d), lambda i: (cur(i), 0)),
            pl.BlockSpec((CONV_HALO, d), lambda i: (jnp.maximum(cur(i) * hb - 1, 0), 0)),
            pl.BlockSpec((bm, d), lambda i: (prev(i), 0)),
            pl.BlockSpec((1, 1, 3 * d), lambda i: (prev(i) // tps, 0, 0)),
            pl.BlockSpec(dw.shape, full),
            pl.BlockSpec((1, d), full),
            pl.BlockSpec((1, d), full),
            pl.BlockSpec((1, d), full),
            _resident((None, n_blk, d, d // n_blk), lambda i: (r, 0, 0, 0)),
            pl.BlockSpec((1, d), full),
            pl.BlockSpec((1, d), full),
        ],
        out_specs=pl.BlockSpec((bm, d), lambda i: (prev(i), 0)),
        out_shape=jax.ShapeDtypeStruct((m, d), F32),
        scratch_shapes=[
            pltpu.VMEM((d // cc, CONV_HALO + bm, cc), F32),
            pltpu.VMEM((bm, d), F32),
            pltpu.VMEM((bm, d), BF16),
            pltpu.VMEM((n_blk, bm, d // n_blk), F32),
            pltpu.VMEM((bm, LANES), F32),
        ],
        compiler_params=_params(("arbitrary",)),
        name="conv_sublayer",
    )(u, u, x, mod, dw, dw_b, ln_g, ln_b, w_out, b_out, g_post)


def kernel(x, c, ada_w, ada_b, norm_pre, norm_post, ffn_w_gate, ffn_w_up, ffn_w_down, ret_w_in, ret_gn,
           ret_w_out, conv_w_in, conv_b_in, conv_dw, conv_dw_b, conv_ln_g, conv_ln_b, conv_w_out, conv_b_out):
    b, s, d = x.shape
    depth = ada_w.shape[0]
    m = b * s
    f = ffn_w_gate.shape[-1]
    t = _tiles(m, s, d, f)
    assert s % t["bm"] == 0 and s % t["tr"] == 0 and t["tr"] % CHUNK == 0 and t["bm"] % CONV_HALO == 0
    assert CONV_HALO >= CONV_WIDTH - 1 and t["bn"] % t["dk"] == 0

    mods = _ada_mods(c, ada_w.reshape(depth * N_SUBLAYERS, d, 3 * d),
                     ada_b.reshape(depth * N_SUBLAYERS, 1, 3 * d), t)
    mods = mods.reshape(depth, N_SUBLAYERS, b, 1, 3 * d)
    row = lambda p: p.reshape(1, -1)
    cos, sin, dmat, tab = _retention_tables(s, t["dk"], t["tr"])
    dw_pad = CONV_HALO - CONV_WIDTH
    wg, wu, wd = ffn_w_gate.astype(BF16), ffn_w_up.astype(BF16), ffn_w_down.astype(BF16)
    ret_in, ret_out = ret_w_in.astype(BF16), ret_w_out.astype(BF16)
    conv_in = conv_w_in.astype(BF16)
    n_blk = t["bm"] // min(CONV_ROW_CHUNK, t["bm"])
    conv_out = conv_w_out.astype(BF16).reshape(-1, d, n_blk, d // n_blk).transpose(0, 2, 1, 3)

    xf = x.reshape(m, d)
    for i in range(depth):
        r = i // 2
        ffn = lambda xin, j, w: _ffn_sublayer(
            xin, mods[i, j], row(norm_pre[i, j]), row(norm_post[i, j]), wg, wu, wd, i, w, 0.5, s,
            t["bm"], t["bf"])
        xf = ffn(xf, 0, 0)
        if i % 2 == 0:
            qk, v, g = _qkvg_proj(xf, mods[i, 1], row(norm_pre[i, 1]), cos, sin, ret_in, r, s, t)
            y = _retention_core(qk, v, g, row(ret_gn[r]), dmat, tab, b, s, t)
            xf = _outproj_sublayer(y, xf, mods[i, 1], row(norm_post[i, 1]), ret_out, r, 1.0, s, t)
        else:
            u = _glu_proj(xf, mods[i, 1], row(norm_pre[i, 1]), conv_in, row(conv_b_in[r]), r, s, t)
            xf = _conv_sublayer(u, xf, mods[i, 1], jnp.pad(conv_dw[r], ((0, dw_pad), (0, 0))), row(conv_dw_b[r]),
                                row(conv_ln_g[r]), row(conv_ln_b[r]), conv_out, row(conv_b_out[r]),
                                row(norm_post[i, 1]), r, 1.0, s, t)
        xf = ffn(xf, 2, 1)
    return xf.reshape(b, s, d)
```

```python
import functools

import jax
import jax.numpy as jnp
from jax import lax
from jax.experimental import pallas as pl
from jax.experimental.pallas import tpu as pltpu

EPS = 1e-6
RET_HEADS = 8
CHUNK = 64
CONV_WIDTH = 31
ROPE_BASE = 10000.0
N_SUBLAYERS = 3

F32 = jnp.float32
BF16 = jnp.bfloat16

V7X_VMEM_LIMIT_BYTES = 60 * 1024 * 1024
LANES = 128
NORM_ROW_CHUNK = 16
CONV_HALO = 32
CONV_ROW_CHUNK = 64
CONV_LANE_CHUNK = LANES
SUBLANES = 8
CONV_TMP_SLOTS = 4


def _tiles(m, s, d, f):
    dk = d // RET_HEADS
    return dict(
        bm=min(512, s),
        bm_qkvg=min(1024, s),
        bf=min(1024, f),
        bn=min(512, d),
        tr=min(256, s),
        ada_bn=min(1024, 3 * d),
        dk=dk,
    )


def _params(sem):
    return pltpu.CompilerParams(dimension_semantics=sem, vmem_limit_bytes=V7X_VMEM_LIMIT_BYTES)


def _resident(block_shape, index_map):
    return pl.BlockSpec(block_shape, index_map, pipeline_mode=pl.Buffered(1))


def _call_with_cast_job(body, *, grid, in_specs, out_specs, out_shape, scratch_shapes, sem, name, args,
                        cast=None, n_steps=None, step_of=None):
    n_in, n_out = len(in_specs), len(out_specs)
    cast = cast or []
    in_specs, out_specs, out_shape, args = list(in_specs), list(out_specs), list(out_shape), list(args)
    for w, (layer, which) in cast:
        r, c = w.shape[-2:]
        assert r % n_steps == 0
        rows = r // n_steps
        in_specs.append(pl.BlockSpec((None, None, rows, c),
                                     lambda *g, layer=layer, which=which: (layer, which, step_of(*g), 0)))
        out_specs.append(pl.BlockSpec((rows, c), lambda *g: (step_of(*g), 0)))
        out_shape.append(jax.ShapeDtypeStruct((r, c), BF16))
        args.append(w)
    n_cast = len(cast)

    def kernel(*refs):
        ins, cast_in = refs[:n_in], refs[n_in:n_in + n_cast]
        outs = refs[n_in + n_cast:n_in + n_cast + n_out]
        cast_out = refs[n_in + n_cast + n_out:n_in + 2 * n_cast + n_out]
        body(*ins, *outs, *refs[n_in + 2 * n_cast + n_out:])
        for src, dst in zip(cast_in, cast_out):
            dst[...] = src[...].astype(dst.dtype)

    res = pl.pallas_call(kernel, grid=grid, in_specs=in_specs, out_specs=out_specs, out_shape=out_shape,
                         scratch_shapes=scratch_shapes, compiler_params=_params(sem), name=name)(*args)
    return res[:n_out], list(res[n_out:])


def _row_loop(n_rows, body, unroll):
    rc = min(NORM_ROW_CHUNK, n_rows)
    n = n_rows // rc

    def step(r, carry):
        body(pl.ds(pl.multiple_of(r * rc, rc), rc))
        return carry

    lax.fori_loop(0, n, step, 0, unroll=min(unroll, n))


def _inv_rms_rows(z_ref, inv_ref, bias_ref=None):
    def body(rows):
        zv = z_ref[rows, :]
        if bias_ref is not None:
            zv = zv + bias_ref[...]
        inv = lax.rsqrt(jnp.mean(zv * zv, axis=-1, keepdims=True) + EPS)
        inv_ref[rows, :] = jnp.broadcast_to(inv, (zv.shape[0], inv_ref.shape[1]))

    _row_loop(z_ref.shape[0], body, unroll=8)


def _premod_rows(x_ref, mod_ref, gpre_ref, h_ref, inv_ref, d):
    _inv_rms_rows(x_ref, inv_ref)
    shift = mod_ref[0, :, 0:d]
    gain = gpre_ref[...] * (1.0 + mod_ref[0, :, d:2 * d])
    lanes = inv_ref.shape[1]

    def body(rows):
        inv = inv_ref[rows, :]
        for j in range(d // lanes):
            cs = slice(j * lanes, (j + 1) * lanes)
            h_ref[rows, cs] = (x_ref[rows, cs] * inv * gain[:, cs] + shift[:, cs]).astype(h_ref.dtype)

    _row_loop(x_ref.shape[0], body, unroll=4)


def _premod_next(xn_ref, modn_ref, gpre_ref, h_ref, slot, row0, d):
    n_rows = xn_ref.shape[0]
    rc = min(NORM_ROW_CHUNK, n_rows)
    shift = modn_ref[0, :, 0:d]
    gain = gpre_ref[...] * (1.0 + modn_ref[0, :, d:2 * d])
    for c in range(n_rows // rc):
        xv = xn_ref[c * rc:(c + 1) * rc, :]
        inv = lax.rsqrt(jnp.mean(xv * xv, axis=-1, keepdims=True) + EPS)
        start = row0 + c * rc
        rows = pl.ds(start if isinstance(start, int) else pl.multiple_of(start, rc), rc)
        h_ref[slot, rows, :] = (xv * inv * gain + shift).astype(h_ref.dtype)


def _post_residual_rows(z_ref, x_ref, mod_ref, gpost_ref, o_ref, inv_ref, res_w, d, bias_ref=None):
    _inv_rms_rows(z_ref, inv_ref, bias_ref)
    gain = (res_w * mod_ref[0, :, 2 * d:3 * d]) * gpost_ref[...]
    lanes = inv_ref.shape[1]

    def body(rows):
        inv = inv_ref[rows, :]
        for j in range(d // lanes):
            cs = slice(j * lanes, (j + 1) * lanes)
            zv = z_ref[rows, cs]
            if bias_ref is not None:
                zv = zv + bias_ref[:, cs]
            o_ref[rows, cs] = x_ref[rows, cs] + zv * inv * gain[:, cs]

    _row_loop(x_ref.shape[0], body, unroll=4)


def _post_residual_slabs(z_ref, x_ref, mod_ref, gpost_ref, o_ref, inv_ref, bias_ref, res_w, d):
    n_slabs, n_rows, zw = z_ref.shape
    lw = min(inv_ref.shape[1], zw)
    gain = (res_w * mod_ref[0, :, 2 * d:3 * d]) * gpost_ref[...]

    def sumsq_body(rows):
        ss = None
        for j in range(n_slabs):
            zv = z_ref[j, rows, :] + bias_ref[:, j * zw:(j + 1) * zw]
            part = jnp.sum(zv * zv, axis=-1, keepdims=True)
            ss = part if ss is None else ss + part
        inv = lax.rsqrt(ss * (1.0 / d) + EPS)
        inv_ref[rows, :] = jnp.broadcast_to(inv, (ss.shape[0], inv_ref.shape[1]))

    _row_loop(n_rows, sumsq_body, unroll=8)

    def out_body(rows):
        inv = inv_ref[rows, 0:lw]
        for j in range(n_slabs):
            for q in range(zw // lw):
                cs = slice(j * zw + q * lw, j * zw + (q + 1) * lw)
                zv = z_ref[j, rows, q * lw:(q + 1) * lw] + bias_ref[:, cs]
                o_ref[rows, cs] = x_ref[rows, cs] + zv * inv * gain[:, cs]

    _row_loop(n_rows, out_body, unroll=4)


def _silu(v):
    return v * jax.nn.sigmoid(v)


def _ada_kernel(c_ref, w_ref, b_ref, o_ref):
    cs = _silu(c_ref[...]).astype(BF16)
    o_ref[0] = jnp.dot(cs, w_ref[0].astype(BF16), preferred_element_type=F32) + b_ref[0]


def _ada_mods(c, ada_w, ada_b, t):
    n, d, e = ada_w.shape
    b = c.shape[0]
    bn = t["ada_bn"]
    return pl.pallas_call(
        _ada_kernel,
        grid=(n, e // bn),
        in_specs=[
            pl.BlockSpec((b, d), lambda i, j: (0, 0)),
            pl.BlockSpec((1, d, bn), lambda i, j: (i, 0, j)),
            pl.BlockSpec((1, 1, bn), lambda i, j: (i, 0, j)),
        ],
        out_specs=pl.BlockSpec((1, b, bn), lambda i, j: (i, 0, j)),
        out_shape=jax.ShapeDtypeStruct((n, b, e), F32),
        compiler_params=_params(("arbitrary", "arbitrary")),
        name="ada_mods",
    )(c, ada_w, ada_b)


def _ffn_kernel(x_ref, xn_ref, mod_ref, modn_ref, gpre_ref, gpost_ref, wg_ref, wu_ref, wd_ref, o_ref,
                h_ref, inv_ref, *, res_w, d):
    i = pl.program_id(0)
    f = pl.program_id(1)
    slot = lax.rem(i, 2)

    @pl.when(jnp.logical_and(i == 0, f == 0))
    def _():
        _premod_rows(x_ref, mod_ref, gpre_ref, h_ref.at[0], inv_ref, d)

    @pl.when(f == 0)
    def _():
        o_ref[...] = jnp.zeros_like(o_ref)

    h = h_ref[slot]
    hg = jnp.dot(h, wg_ref[...], preferred_element_type=F32)
    hu = jnp.dot(h, wu_ref[...], preferred_element_type=F32)
    a = (_silu(hg) * hu).astype(BF16)
    _premod_next(xn_ref, modn_ref, gpre_ref, h_ref, 1 - slot, f * xn_ref.shape[0], d)
    o_ref[...] += jnp.dot(a, wd_ref[...], preferred_element_type=F32)

    @pl.when(f == pl.num_programs(1) - 1)
    def _():
        _post_residual_rows(o_ref, x_ref, mod_ref, gpost_ref, o_ref, inv_ref, res_w, d)


def _ffn_sublayer(x, mod, g_pre, g_post, wg, wu, wd, res_w, s, bm, bf):
    m, d = x.shape
    f = wg.shape[-1]
    tps = s // bm
    n_tiles, nf = m // bm, f // bf
    rps = bm // nf
    assert bm % nf == 0 and rps % min(NORM_ROW_CHUNK, rps) == 0
    nxt = lambda i: jnp.minimum(i + 1, n_tiles - 1)
    return pl.pallas_call(
        functools.partial(_ffn_kernel, res_w=res_w, d=d),
        grid=(n_tiles, nf),
        in_specs=[
            pl.BlockSpec((bm, d), lambda i, j: (i, 0)),
            pl.BlockSpec((rps, d), lambda i, j: (nxt(i) * nf + j, 0)),
            pl.BlockSpec((1, 1, 3 * d), lambda i, j: (i // tps, 0, 0)),
            pl.BlockSpec((1, 1, 3 * d), lambda i, j: (nxt(i) // tps, 0, 0)),
            pl.BlockSpec((1, d), lambda i, j: (0, 0)),
            pl.BlockSpec((1, d), lambda i, j: (0, 0)),
            pl.BlockSpec((d, bf), lambda i, j: (0, j)),
            pl.BlockSpec((d, bf), lambda i, j: (0, j)),
            pl.BlockSpec((bf, d), lambda i, j: (j, 0)),
        ],
        out_specs=pl.BlockSpec((bm, d), lambda i, j: (i, 0)),
        out_shape=jax.ShapeDtypeStruct((m, d), F32),
        scratch_shapes=[pltpu.VMEM((2, bm, d), BF16), pltpu.VMEM((bm, LANES), F32)],
        compiler_params=_params(("arbitrary", "arbitrary")),
        name="ffn_sublayer",
    )(x, x, mod, mod, g_pre, g_post, wg, wu, wd)


def _qkvg_kernel(x0_ref, xn_ref, mod0_ref, modn_ref, gpre_ref, cos_ref, sin_ref, wqk_ref, wv_ref, wg_ref,
                 qk_ref, v_ref, g_ref, h_ref, inv_ref, *, d, dk, n_q):
    i = pl.program_id(0)
    j = pl.program_id(1)
    slot = lax.rem(i, 2)

    @pl.when(jnp.logical_and(i == 0, j == 0))
    def _():
        _premod_rows(x0_ref, mod0_ref, gpre_ref, h_ref.at[0], inv_ref, d)

    h = h_ref[slot]
    g_ref[...] = _silu(jnp.dot(h, wg_ref[...], preferred_element_type=F32))
    y = jnp.dot(h, wqk_ref[...], preferred_element_type=F32)
    _premod_next(xn_ref, modn_ref, gpre_ref, h_ref, 1 - slot, j * xn_ref.shape[0], d)
    v_ref[...] = jnp.dot(h, wv_ref[...], preferred_element_type=F32).astype(BF16)

    half = dk // 2
    sc = jnp.where(j >= n_q, dk ** -0.5, 1.0).astype(F32)
    cos = cos_ref[...] * sc
    sin = sin_ref[...] * sc
    for hh in range(y.shape[1] // dk):
        t1 = y[:, hh * dk:hh * dk + half]
        t2 = y[:, hh * dk + half:(hh + 1) * dk]
        qk_ref[:, hh * dk:hh * dk + half] = (t1 * cos - t2 * sin).astype(BF16)
        qk_ref[:, hh * dk + half:(hh + 1) * dk] = (t1 * sin + t2 * cos).astype(BF16)


def _qkvg_proj(x, mod, g_pre, cos, sin, w_in, r, s, t):
    m, d = x.shape
    bm, bn, dk = t["bm_qkvg"], t["bn"], t["dk"]
    tps = s // bm
    vw = 2 * d
    nj = vw // bn
    n_tiles = m // bm
    rps = bm // nj
    assert bm % nj == 0 and rps % min(NORM_ROW_CHUNK, rps) == 0
    nxt = lambda i: jnp.minimum(i + 1, n_tiles - 1)
    half = dk // 2
    col = lambda i, j: (i, j)
    return pl.pallas_call(
        functools.partial(_qkvg_kernel, d=d, dk=dk, n_q=d // bn),
        grid=(n_tiles, nj),
        in_specs=[
            _resident((bm, d), lambda i, j: (0, 0)),
            pl.BlockSpec((rps, d), lambda i, j: (nxt(i) * nj + j, 0)),
            _resident((1, 1, 3 * d), lambda i, j: (0, 0, 0)),
            pl.BlockSpec((1, 1, 3 * d), lambda i, j: (nxt(i) // tps, 0, 0)),
            pl.BlockSpec((1, d), lambda i, j: (0, 0)),
            pl.BlockSpec((bm, half), lambda i, j: (i % tps, 0)),
            pl.BlockSpec((bm, half), lambda i, j: (i % tps, 0)),
            pl.BlockSpec((None, d, bn), lambda i, j: (r, 0, j)),
            pl.BlockSpec((None, d, bn), lambda i, j: (r, 0, nj + j)),
            pl.BlockSpec((None, d, bn), lambda i, j: (r, 0, 2 * nj + j)),
        ],
        out_specs=[pl.BlockSpec((bm, bn), col), pl.BlockSpec((bm, bn), col), pl.BlockSpec((bm, bn), col)],
        out_shape=[
            jax.ShapeDtypeStruct((m, 2 * d), BF16),
            jax.ShapeDtypeStruct((m, vw), BF16),
            jax.ShapeDtypeStruct((m, vw), F32),
        ],
        scratch_shapes=[pltpu.VMEM((2, bm, d), BF16), pltpu.VMEM((bm, LANES), F32)],
        compiler_params=_params(("arbitrary", "arbitrary")),
        name="ret_qkvg_proj",
    )(x, x, mod, mod, g_pre, cos, sin, w_in, w_in, w_in)


def _retention_kernel(q_ref, k_ref, v_ref, g_ref, gn_ref, dmat_ref, tab_ref, o_ref, state_ref, *, tr):
    tab = tab_ref[0]
    qdec, kdec, cdec = tab[:, 0:1], tab[:, 1:2], tab[0:1, 2:3]
    state_ref[...] = jnp.zeros_like(state_ref)

    for blk in range(q_ref.shape[0] // tr):
        rows = slice(blk * tr, (blk + 1) * tr)
        q = q_ref[rows, :]
        k = k_ref[rows, :]
        v = v_ref[rows, :]
        state = state_ref[...]

        scores = lax.dot_general(q, k, (((1,), (1,)), ((), ())), preferred_element_type=F32)
        p = (scores * dmat_ref[0]).astype(BF16)
        o = jnp.dot(p, v, preferred_element_type=F32)
        o = o + qdec * jnp.dot(q, state.astype(BF16), preferred_element_type=F32)

        kd = (k.astype(F32) * kdec).astype(BF16)
        state_ref[...] = state * cdec + lax.dot_general(
            kd, v, (((0,), (0,)), ((), ())), preferred_element_type=F32)

        mu = jnp.mean(o, axis=-1, keepdims=True)
        oc = o - mu
        var = jnp.mean(oc * oc, axis=-1, keepdims=True)
        on = oc * lax.rsqrt(var + EPS) * gn_ref[...]
        o_ref[rows, :] = (g_ref[rows, :] * on).astype(BF16)


def _retention_core(qk, v, g, gn_g, dmat, tab, b, s, t, cast=None):
    m = qk.shape[0]
    h = RET_HEADS
    tr = t["tr"]
    dk = qk.shape[1] // (2 * h)
    dv = v.shape[1] // h
    return _call_with_cast_job(
        functools.partial(_retention_kernel, tr=tr),
        cast=cast, n_steps=b * h, step_of=lambda bi, hi: bi * h + hi,
        grid=(b, h),
        in_specs=[
            pl.BlockSpec((s, dk), lambda bi, hi: (bi, hi)),
            pl.BlockSpec((s, dk), lambda bi, hi: (bi, h + hi)),
            pl.BlockSpec((s, dv), lambda bi, hi: (bi, hi)),
            pl.BlockSpec((s, dv), lambda bi, hi: (bi, hi)),
            pl.BlockSpec((1, dv), lambda bi, hi: (0, hi)),
            pl.BlockSpec((1, tr, tr), lambda bi, hi: (hi, 0, 0)),
            pl.BlockSpec((1, tr, 128), lambda bi, hi: (hi, 0, 0)),
        ],
        out_specs=[pl.BlockSpec((s, dv), lambda bi, hi: (bi, hi))],
        out_shape=[jax.ShapeDtypeStruct((m, h * dv), BF16)],
        scratch_shapes=[pltpu.VMEM((dk, dv), F32)],
        sem=("parallel", "parallel"),
        name="ret_core",
        args=(qk, qk, v, g, gn_g, dmat, tab))


def _retention_tables(s, dk, tr):
    half = dk // 2
    inv = ROPE_BASE ** (-jnp.arange(half, dtype=F32) / half)
    ang = jnp.arange(s, dtype=F32)[:, None] * inv[None, :]
    gamma = 1.0 - jnp.exp2(-5.0 - jnp.arange(RET_HEADS, dtype=F32))
    log_g = jnp.log(gamma)
    r = jnp.arange(tr, dtype=F32)
    n, mm = r[:, None], r[None, :]
    cn, cm = jnp.floor(n / CHUNK), jnp.floor(mm / CHUNK)
    dmat = jnp.where(cm <= cn, jnp.exp(log_g[:, None, None] * jnp.abs(n - mm)), 0.0)
    tab = jnp.zeros((RET_HEADS, tr, 128), F32)
    tab = tab.at[:, :, 0].set(jnp.exp(log_g[:, None] * (r[None, :] + 1.0)))
    tab = tab.at[:, :, 1].set(jnp.exp(log_g[:, None] * (tr - 1.0 - r[None, :])))
    tab = tab.at[:, :, 2].set(jnp.exp(log_g * tr)[:, None])
    return jnp.cos(ang), jnp.sin(ang), dmat, tab


def _outproj_kernel(y_ref, x_ref, mod_ref, gpost_ref, w_ref, o_ref, inv_ref, *, res_w, d):
    o_ref[...] = jnp.dot(y_ref[...], w_ref[...], preferred_element_type=F32)
    _post_residual_rows(o_ref, x_ref, mod_ref, gpost_ref, o_ref, inv_ref, res_w, d)


def _outproj_sublayer(y, x, mod, g_post, w, r, res_w, s, t, cast=None):
    m, d = x.shape
    kk = y.shape[1]
    bm = t["bm"]
    tps = s // bm
    return _call_with_cast_job(
        functools.partial(_outproj_kernel, res_w=res_w, d=d),
        cast=cast, n_steps=m // bm, step_of=lambda i: i,
        grid=(m // bm,),
        in_specs=[
            pl.BlockSpec((bm, kk), lambda i: (i, 0)),
            pl.BlockSpec((bm, d), lambda i: (i, 0)),
            pl.BlockSpec((1, 1, 3 * d), lambda i: (i // tps, 0, 0)),
            pl.BlockSpec((1, d), lambda i: (0, 0)),
            _resident((None, kk, d), lambda i: (r, 0, 0)),
        ],
        out_specs=[pl.BlockSpec((bm, d), lambda i: (i, 0))],
        out_shape=[jax.ShapeDtypeStruct((m, d), F32)],
        scratch_shapes=[pltpu.VMEM((bm, LANES), F32)],
        sem=("parallel",),
        name="ret_outproj",
        args=(y, x, mod, g_post, w))


def _glu_kernel(x0_ref, xn_ref, mod0_ref, modn_ref, gpre_ref, w_ref, b_ref, u_ref, h_ref, inv_ref, *, d, bn):
    i = pl.program_id(0)
    slot = lax.rem(i, 2)

    @pl.when(i == 0)
    def _():
        _premod_rows(x0_ref, mod0_ref, gpre_ref, h_ref.at[0], inv_ref, d)

    h = h_ref[slot]
    for j in range(d // bn):
        ca, cb = slice(j * bn, (j + 1) * bn), slice(d + j * bn, d + (j + 1) * bn)
        a = jnp.dot(h, w_ref[:, ca], preferred_element_type=F32) + b_ref[:, ca]
        b = jnp.dot(h, w_ref[:, cb], preferred_element_type=F32) + b_ref[:, cb]
        u_ref[:, ca] = a * jax.nn.sigmoid(b)

    _premod_next(xn_ref, modn_ref, gpre_ref, h_ref, 1 - slot, 0, d)


def _glu_proj(x, mod, g_pre, w_in, b_in, r, s, t):
    m, d = x.shape
    bm, bn = t["bm"], t["bn"]
    tps = s // bm
    n_tiles = m // bm
    nxt = lambda i: jnp.minimum(i + 1, n_tiles - 1)
    return pl.pallas_call(
        functools.partial(_glu_kernel, d=d, bn=bn),
        grid=(n_tiles,),
        in_specs=[
            _resident((bm, d), lambda i: (0, 0)),
            pl.BlockSpec((bm, d), lambda i: (nxt(i), 0)),
            _resident((1, 1, 3 * d), lambda i: (0, 0, 0)),
            pl.BlockSpec((1, 1, 3 * d), lambda i: (nxt(i) // tps, 0, 0)),
            pl.BlockSpec((1, d), lambda i: (0, 0)),
            _resident((None, d, 2 * d), lambda i: (r, 0, 0)),
            pl.BlockSpec((1, 2 * d), lambda i: (0, 0)),
        ],
        out_specs=pl.BlockSpec((bm, d), lambda i: (i, 0)),
        out_shape=jax.ShapeDtypeStruct((m, d), F32),
        scratch_shapes=[pltpu.VMEM((2, bm, d), BF16), pltpu.VMEM((bm, LANES), F32)],
        compiler_params=_params(("arbitrary",)),
        name="conv_glu_proj",
    )(x, x, mod, mod, g_pre, w_in, b_in)


def _conv_kernel(ucur_ref, uprev_ref, x_ref, mod_ref, dw_ref, dwb_ref, lng_ref, lnb_ref, w_ref,
                 bout_ref, gpost_ref, o_ref, ubuf_ref, cbuf_ref, hc_ref, z_ref, inv_ref, tmp_ref,
                 *, res_w, d, tps, n_tiles, rc, cc):
    i = pl.program_id(0)
    bm = ucur_ref.shape[0]

    @pl.when(i == 0)
    def _():
        hc_ref[...] = jnp.zeros_like(hc_ref)

    first = (jnp.minimum(i, n_tiles - 1) % tps) == 0
    for c in range(d // cc):
        cols = slice(c * cc, (c + 1) * cc)
        ubuf_ref[c, 0:CONV_HALO, :] = jnp.where(first, 0.0, uprev_ref[:, cols])
        ubuf_ref[c, CONV_HALO:CONV_HALO + bm, :] = ucur_ref[:, cols]
    off = CONV_HALO - (CONV_WIDTH - 1)

    def blk_body(blk, carry):
        base = pl.multiple_of(blk * rc, rc)
        for c in range(d // cc):
            cols = slice(c * cc, (c + 1) * cc)
            win = ubuf_ref[c, pl.ds(base, rc + CONV_HALO), :]
            acc = jnp.broadcast_to(dwb_ref[:, cols], (rc, cc))
            for phase in range(SUBLANES):
                n_rows = rc if phase == 0 else rc + SUBLANES
                part = None
                for tap in range(CONV_WIDTH):
                    o = off + tap
                    if o % SUBLANES == phase:
                        term = win[o - phase:o - phase + n_rows] * dw_ref[tap:tap + 1, cols]
                        part = term if part is None else part + term
                if phase == 0:
                    acc = acc + part
                else:
                    tmp_ref[c % CONV_TMP_SLOTS, phase] = part
                    acc = acc + tmp_ref[c % CONV_TMP_SLOTS, phase, phase:phase + rc, :]
            cbuf_ref[pl.ds(base, rc), cols] = acc
        z_ref[blk] = jnp.dot(hc_ref[...], w_ref[blk], preferred_element_type=F32)
        return carry

    lax.fori_loop(0, bm // rc, blk_body, 0)

    @pl.when(i > 0)
    def _():
        _post_residual_slabs(z_ref, x_ref, mod_ref, gpost_ref, o_ref, inv_ref, bout_ref, res_w, d)

    cv = cbuf_ref[...]
    mu = jnp.mean(cv, axis=-1, keepdims=True)
    cz = cv - mu
    var = jnp.mean(cz * cz, axis=-1, keepdims=True)
    yn = cz * lax.rsqrt(var + EPS) * lng_ref[...] + lnb_ref[...]
    hc_ref[...] = _silu(yn).astype(BF16)


def _conv_sublayer(u, x, mod, dw, dw_b, ln_g, ln_b, w_out, b_out, g_post, r, res_w, s, t, cast=None):
    m, d = x.shape
    bm = t["bm"]
    tps = s // bm
    n_tiles = m // bm
    hb = bm // CONV_HALO
    rc, cc = min(CONV_ROW_CHUNK, bm), min(CONV_LANE_CHUNK, d)
    n_blk = bm // rc
    assert w_out.shape[1:] == (n_blk, d, d // n_blk)
    full = lambda i: (0, 0)
    cur = lambda i: jnp.minimum(i, n_tiles - 1)
    prev = lambda i: jnp.maximum(i - 1, 0)
    return _call_with_cast_job(
        functools.partial(_conv_kernel, res_w=res_w, d=d, tps=tps, n_tiles=n_tiles, rc=rc, cc=cc),
        cast=cast, n_steps=n_tiles, step_of=cur,
        grid=(n_tiles + 1,),
        in_specs=[
            pl.BlockSpec((bm, d), lambda i: (cur(i), 0)),
            pl.BlockSpec((CONV_HALO, d), lambda i: (jnp.maximum(cur(i) * hb - 1, 0), 0)),
            pl.BlockSpec((bm, d), lambda i: (prev(i), 0)),
            pl.BlockSpec((1, 1, 3 * d), lambda i: (prev(i) // tps, 0, 0)),
            pl.BlockSpec(dw.shape, full),
            pl.BlockSpec((1, d), full),
            pl.BlockSpec((1, d), full),
            pl.BlockSpec((1, d), full),
            _resident((None, n_blk, d, d // n_blk), lambda i: (r, 0, 0, 0)),
            pl.BlockSpec((1, d), full),
            pl.BlockSpec((1, d), full),
        ],
        out_specs=[pl.BlockSpec((bm, d), lambda i: (prev(i), 0))],
        out_shape=[jax.ShapeDtypeStruct((m, d), F32)],
        scratch_shapes=[
            pltpu.VMEM((d // cc, CONV_HALO + bm, cc), F32),
            pltpu.VMEM((bm, d), F32),
            pltpu.VMEM((bm, d), BF16),
            pltpu.VMEM((n_blk, bm, d // n_blk), F32),
            pltpu.VMEM((bm, LANES), F32),
            pltpu.VMEM((CONV_TMP_SLOTS, SUBLANES, rc + SUBLANES, cc), F32),
        ],
        sem=("arbitrary",),
        name="conv_sublayer",
        args=(u, u, x, mod, dw, dw_b, ln_g, ln_b, w_out, b_out, g_post))


def kernel(x, c, ada_w, ada_b, norm_pre, norm_post, ffn_w_gate, ffn_w_up, ffn_w_down, ret_w_in, ret_gn,
           ret_w_out, conv_w_in, conv_b_in, conv_dw, conv_dw_b, conv_ln_g, conv_ln_b, conv_w_out, conv_b_out):
    b, s, d = x.shape
    depth = ada_w.shape[0]
    m = b * s
    f = ffn_w_gate.shape[-1]
    t = _tiles(m, s, d, f)
    assert s % t["bm"] == 0 and s % t["tr"] == 0 and t["tr"] % CHUNK == 0 and t["bm"] % CONV_HALO == 0
    assert CONV_HALO >= CONV_WIDTH - 1 and t["bn"] % t["dk"] == 0

    mods = _ada_mods(c, ada_w.reshape(depth * N_SUBLAYERS, d, 3 * d),
                     ada_b.reshape(depth * N_SUBLAYERS, 1, 3 * d), t)
    mods = mods.reshape(depth, N_SUBLAYERS, b, 1, 3 * d)
    row = lambda p: p.reshape(1, -1)
    cos, sin, dmat, tab = _retention_tables(s, t["dk"], t["tr"])
    dw_pad = CONV_HALO - CONV_WIDTH
    ffn_ws = [ffn_w_gate, ffn_w_up, ffn_w_down]
    ret_in, ret_out = ret_w_in.astype(BF16), ret_w_out.astype(BF16)
    conv_in = conv_w_in.astype(BF16)
    n_blk = t["bm"] // min(CONV_ROW_CHUNK, t["bm"])
    conv_out = conv_w_out.astype(BF16).reshape(-1, d, n_blk, d // n_blk).transpose(0, 2, 1, 3)

    ffn_bf16 = {}

    def ffn_weights(key):
        if key not in ffn_bf16:
            ffn_bf16[key] = [w[key].astype(BF16) for w in ffn_ws]
        return ffn_bf16[key]

    def cast_jobs(keys):
        keys = [k for k in keys if k[0] < depth and k not in ffn_bf16]
        return keys, [(w, k) for k in keys for w in ffn_ws]

    def store_cast(keys, mats):
        for n, k in enumerate(keys):
            ffn_bf16[k] = mats[n * len(ffn_ws):(n + 1) * len(ffn_ws)]

    xf = x.reshape(m, d)
    for i in range(depth):
        r = i // 2
        ffn = lambda xin, j, ws: _ffn_sublayer(
            xin, mods[i, j], row(norm_pre[i, j]), row(norm_post[i, j]), *ws, 0.5, s, t["bm"], t["bf"])
        xf = ffn(xf, 0, ffn_weights((i, 0)))
        if i % 2 == 0:
            qk, v, g = _qkvg_proj(xf, mods[i, 1], row(norm_pre[i, 1]), cos, sin, ret_in, r, s, t)
            keys, jobs = cast_jobs([(i, 1), (i + 1, 0)])
            (y,), mats = _retention_core(qk, v, g, row(ret_gn[r]), dmat, tab, b, s, t, cast=jobs)
            store_cast(keys, mats)
            keys, jobs = cast_jobs([(i + 1, 1)])
            (xf,), mats = _outproj_sublayer(y, xf, mods[i, 1], row(norm_post[i, 1]), ret_out, r, 1.0, s, t,
                                            cast=jobs)
            store_cast(keys, mats)
        else:
            u = _glu_proj(xf, mods[i, 1], row(norm_pre[i, 1]), conv_in, row(conv_b_in[r]), r, s, t)
            (xf,), _ = _conv_sublayer(
                u, xf, mods[i, 1], jnp.pad(conv_dw[r], ((0, dw_pad), (0, 0))), row(conv_dw_b[r]),
                row(conv_ln_g[r]), row(conv_ln_b[r]), conv_out, row(conv_b_out[r]), row(norm_post[i, 1]),
                r, 1.0, s, t)
        xf = ffn(xf, 2, ffn_weights((i, 1)))
    return xf.reshape(b, s, d)
```

```python
import functools

import jax
import jax.numpy as jnp
from jax import lax
from jax.experimental import pallas as pl
from jax.experimental.pallas import tpu as pltpu

EPS = 1e-6
RET_HEADS = 8
CHUNK = 64
CONV_WIDTH = 31
ROPE_BASE = 10000.0
N_SUBLAYERS = 3

F32 = jnp.float32
BF16 = jnp.bfloat16

V7X_VMEM_LIMIT_BYTES = 60 * 1024 * 1024
LANES = 128
NORM_ROW_CHUNK = 16
CONV_HALO = 32
CONV_ROW_CHUNK = 64
CONV_LANE_CHUNK = LANES


def _tiles(m, s, d, f):
    dk = d // RET_HEADS
    return dict(
        bm=min(512, s),
        bm_qkvg=min(1024, s),
        bf=min(1024, f),
        bn=min(512, d),
        tr=min(256, s),
        ada_bn=min(1024, 3 * d),
        dk=dk,
    )


def _params(sem):
    return pltpu.CompilerParams(dimension_semantics=sem, vmem_limit_bytes=V7X_VMEM_LIMIT_BYTES)


def _resident(block_shape, index_map):
    return pl.BlockSpec(block_shape, index_map, pipeline_mode=pl.Buffered(1))


def _call_with_cast_job(body, *, grid, in_specs, out_specs, out_shape, scratch_shapes, sem, name, args,
                        cast=None, cast_grid=None, step_of=None):
    n_in, n_out = len(in_specs), len(out_specs)
    cast = cast or []
    in_specs, out_specs, out_shape, args = list(in_specs), list(out_specs), list(out_shape), list(args)
    for w, lead in cast:
        r, c = w.shape[-2:]
        assert r % cast_grid[0] == 0 and c % cast_grid[1] == 0 and len(lead) == w.ndim - 2
        rows, cols = r // cast_grid[0], c // cast_grid[1]
        in_specs.append(pl.BlockSpec((None,) * len(lead) + (rows, cols), lambda *g, lead=lead: lead + step_of(*g)))
        out_specs.append(pl.BlockSpec((rows, cols), lambda *g: step_of(*g)))
        out_shape.append(jax.ShapeDtypeStruct((r, c), BF16))
        args.append(w)
    n_cast = len(cast)

    def kernel(*refs):
        ins, cast_in = refs[:n_in], refs[n_in:n_in + n_cast]
        outs = refs[n_in + n_cast:n_in + n_cast + n_out]
        cast_out = refs[n_in + n_cast + n_out:n_in + 2 * n_cast + n_out]
        body(*ins, *outs, *refs[n_in + 2 * n_cast + n_out:])
        for src, dst in zip(cast_in, cast_out):
            dst[...] = src[...].astype(dst.dtype)

    res = pl.pallas_call(kernel, grid=grid, in_specs=in_specs, out_specs=out_specs, out_shape=out_shape,
                         scratch_shapes=scratch_shapes, compiler_params=_params(sem), name=name)(*args)
    return res[:n_out], list(res[n_out:])


def _row_loop(n_rows, body, unroll):
    rc = min(NORM_ROW_CHUNK, n_rows)
    n = n_rows // rc

    def step(r, carry):
        body(pl.ds(pl.multiple_of(r * rc, rc), rc))
        return carry

    lax.fori_loop(0, n, step, 0, unroll=min(unroll, n))


def _inv_rms_rows(z_ref, inv_ref, bias_ref=None):
    def body(rows):
        zv = z_ref[rows, :]
        if bias_ref is not None:
            zv = zv + bias_ref[...]
        inv = lax.rsqrt(jnp.mean(zv * zv, axis=-1, keepdims=True) + EPS)
        inv_ref[rows, :] = jnp.broadcast_to(inv, (zv.shape[0], inv_ref.shape[1]))

    _row_loop(z_ref.shape[0], body, unroll=8)


def _premod_rows(x_ref, mod_ref, gpre_ref, h_ref, inv_ref, d):
    _inv_rms_rows(x_ref, inv_ref)
    shift = mod_ref[0, :, 0:d]
    gain = gpre_ref[...] * (1.0 + mod_ref[0, :, d:2 * d])
    lanes = inv_ref.shape[1]

    def body(rows):
        inv = inv_ref[rows, :]
        for j in range(d // lanes):
            cs = slice(j * lanes, (j + 1) * lanes)
            h_ref[rows, cs] = (x_ref[rows, cs] * inv * gain[:, cs] + shift[:, cs]).astype(h_ref.dtype)

    _row_loop(x_ref.shape[0], body, unroll=4)


def _premod_next(xn_ref, modn_ref, gpre_ref, h_ref, slot, row0, d):
    n_rows = xn_ref.shape[0]
    rc = min(NORM_ROW_CHUNK, n_rows)
    shift = modn_ref[0, :, 0:d]
    gain = gpre_ref[...] * (1.0 + modn_ref[0, :, d:2 * d])
    for c in range(n_rows // rc):
        xv = xn_ref[c * rc:(c + 1) * rc, :]
        inv = lax.rsqrt(jnp.mean(xv * xv, axis=-1, keepdims=True) + EPS)
        start = row0 + c * rc
        rows = pl.ds(start if isinstance(start, int) else pl.multiple_of(start, rc), rc)
        h_ref[slot, rows, :] = (xv * inv * gain + shift).astype(h_ref.dtype)


def _post_residual_rows(z_ref, x_ref, mod_ref, gpost_ref, o_ref, inv_ref, res_w, d, bias_ref=None):
    _inv_rms_rows(z_ref, inv_ref, bias_ref)
    gain = (res_w * mod_ref[0, :, 2 * d:3 * d]) * gpost_ref[...]
    lanes = inv_ref.shape[1]

    def body(rows):
        inv = inv_ref[rows, :]
        for j in range(d // lanes):
            cs = slice(j * lanes, (j + 1) * lanes)
            zv = z_ref[rows, cs]
            if bias_ref is not None:
                zv = zv + bias_ref[:, cs]
            o_ref[rows, cs] = x_ref[rows, cs] + zv * inv * gain[:, cs]

    _row_loop(x_ref.shape[0], body, unroll=4)


def _post_residual_slabs(z_ref, x_ref, mod_ref, gpost_ref, o_ref, inv_ref, bias_ref, res_w, d):
    n_slabs, n_rows, zw = z_ref.shape
    lw = min(inv_ref.shape[1], zw)
    gain = (res_w * mod_ref[0, :, 2 * d:3 * d]) * gpost_ref[...]

    def sumsq_body(rows):
        ss = None
        for j in range(n_slabs):
            zv = z_ref[j, rows, :] + bias_ref[:, j * zw:(j + 1) * zw]
            part = jnp.sum(zv * zv, axis=-1, keepdims=True)
            ss = part if ss is None else ss + part
        inv = lax.rsqrt(ss * (1.0 / d) + EPS)
        inv_ref[rows, :] = jnp.broadcast_to(inv, (ss.shape[0], inv_ref.shape[1]))

    _row_loop(n_rows, sumsq_body, unroll=8)

    def out_body(rows):
        inv = inv_ref[rows, 0:lw]
        for j in range(n_slabs):
            for q in range(zw // lw):
                cs = slice(j * zw + q * lw, j * zw + (q + 1) * lw)
                zv = z_ref[j, rows, q * lw:(q + 1) * lw] + bias_ref[:, cs]
                o_ref[rows, cs] = x_ref[rows, cs] + zv * inv * gain[:, cs]

    _row_loop(n_rows, out_body, unroll=4)


def _silu(v):
    return v * jax.nn.sigmoid(v)


def _ada_kernel(c_ref, w_ref, b_ref, o_ref):
    cs = _silu(c_ref[...]).astype(BF16)
    o_ref[0] = jnp.dot(cs, w_ref[0].astype(BF16), preferred_element_type=F32) + b_ref[0]


def _ada_mods(c, ada_w, ada_b, t):
    n, d, e = ada_w.shape
    b = c.shape[0]
    bn = t["ada_bn"]
    return pl.pallas_call(
        _ada_kernel,
        grid=(n, e // bn),
        in_specs=[
            pl.BlockSpec((b, d), lambda i, j: (0, 0)),
            pl.BlockSpec((1, d, bn), lambda i, j: (i, 0, j)),
            pl.BlockSpec((1, 1, bn), lambda i, j: (i, 0, j)),
        ],
        out_specs=pl.BlockSpec((1, b, bn), lambda i, j: (i, 0, j)),
        out_shape=jax.ShapeDtypeStruct((n, b, e), F32),
        compiler_params=_params(("arbitrary", "arbitrary")),
        name="ada_mods",
    )(c, ada_w, ada_b)


def _ffn_kernel(x_ref, xn_ref, mod_ref, modn_ref, gpre_ref, gpost_ref, wg_ref, wu_ref, wd_ref, o_ref,
                h_ref, inv_ref, *, res_w, d):
    i = pl.program_id(0)
    f = pl.program_id(1)
    slot = lax.rem(i, 2)

    @pl.when(jnp.logical_and(i == 0, f == 0))
    def _():
        _premod_rows(x_ref, mod_ref, gpre_ref, h_ref.at[0], inv_ref, d)

    @pl.when(f == 0)
    def _():
        o_ref[...] = jnp.zeros_like(o_ref)

    h = h_ref[slot]
    hg = jnp.dot(h, wg_ref[...], preferred_element_type=F32)
    hu = jnp.dot(h, wu_ref[...], preferred_element_type=F32)
    a = (_silu(hg) * hu).astype(BF16)
    _premod_next(xn_ref, modn_ref, gpre_ref, h_ref, 1 - slot, f * xn_ref.shape[0], d)
    o_ref[...] += jnp.dot(a, wd_ref[...], preferred_element_type=F32)

    @pl.when(f == pl.num_programs(1) - 1)
    def _():
        _post_residual_rows(o_ref, x_ref, mod_ref, gpost_ref, o_ref, inv_ref, res_w, d)


def _ffn_sublayer(x, mod, g_pre, g_post, wg, wu, wd, res_w, s, bm, bf, cast=None):
    m, d = x.shape
    f = wg.shape[-1]
    tps = s // bm
    n_tiles, nf = m // bm, f // bf
    rps = bm // nf
    assert bm % nf == 0 and rps % min(NORM_ROW_CHUNK, rps) == 0
    nxt = lambda i: jnp.minimum(i + 1, n_tiles - 1)
    return _call_with_cast_job(
        functools.partial(_ffn_kernel, res_w=res_w, d=d),
        cast=cast, cast_grid=(n_tiles, nf), step_of=lambda i, j: (i, j),
        grid=(n_tiles, nf),
        in_specs=[
            pl.BlockSpec((bm, d), lambda i, j: (i, 0)),
            pl.BlockSpec((rps, d), lambda i, j: (nxt(i) * nf + j, 0)),
            pl.BlockSpec((1, 1, 3 * d), lambda i, j: (i // tps, 0, 0)),
            pl.BlockSpec((1, 1, 3 * d), lambda i, j: (nxt(i) // tps, 0, 0)),
            pl.BlockSpec((1, d), lambda i, j: (0, 0)),
            pl.BlockSpec((1, d), lambda i, j: (0, 0)),
            pl.BlockSpec((d, bf), lambda i, j: (0, j)),
            pl.BlockSpec((d, bf), lambda i, j: (0, j)),
            pl.BlockSpec((bf, d), lambda i, j: (j, 0)),
        ],
        out_specs=[pl.BlockSpec((bm, d), lambda i, j: (i, 0))],
        out_shape=[jax.ShapeDtypeStruct((m, d), F32)],
        scratch_shapes=[pltpu.VMEM((2, bm, d), BF16), pltpu.VMEM((bm, LANES), F32)],
        sem=("arbitrary", "arbitrary"),
        name="ffn_sublayer",
        args=(x, x, mod, mod, g_pre, g_post, wg, wu, wd))


def _qkvg_kernel(x0_ref, xn_ref, mod0_ref, modn_ref, gpre_ref, cos_ref, sin_ref, wqk_ref, wv_ref, wg_ref,
                 qk_ref, v_ref, g_ref, h_ref, inv_ref, *, d, dk, n_q):
    i = pl.program_id(0)
    j = pl.program_id(1)
    slot = lax.rem(i, 2)

    @pl.when(jnp.logical_and(i == 0, j == 0))
    def _():
        _premod_rows(x0_ref, mod0_ref, gpre_ref, h_ref.at[0], inv_ref, d)

    h = h_ref[slot]
    g_ref[...] = _silu(jnp.dot(h, wg_ref[...], preferred_element_type=F32))
    y = jnp.dot(h, wqk_ref[...], preferred_element_type=F32)
    _premod_next(xn_ref, modn_ref, gpre_ref, h_ref, 1 - slot, j * xn_ref.shape[0], d)
    v_ref[...] = jnp.dot(h, wv_ref[...], preferred_element_type=F32).astype(BF16)

    half = dk // 2
    sc = jnp.where(j >= n_q, dk ** -0.5, 1.0).astype(F32)
    cos = cos_ref[...] * sc
    sin = sin_ref[...] * sc
    for hh in range(y.shape[1] // dk):
        t1 = y[:, hh * dk:hh * dk + half]
        t2 = y[:, hh * dk + half:(hh + 1) * dk]
        qk_ref[:, hh * dk:hh * dk + half] = (t1 * cos - t2 * sin).astype(BF16)
        qk_ref[:, hh * dk + half:(hh + 1) * dk] = (t1 * sin + t2 * cos).astype(BF16)


def _qkvg_proj(x, mod, g_pre, cos, sin, w_in, s, t, cast=None):
    m, d = x.shape
    bm, bn, dk = t["bm_qkvg"], t["bn"], t["dk"]
    tps = s // bm
    vw = 2 * d
    nj = vw // bn
    n_tiles = m // bm
    rps = bm // nj
    assert bm % nj == 0 and rps % min(NORM_ROW_CHUNK, rps) == 0
    nxt = lambda i: jnp.minimum(i + 1, n_tiles - 1)
    half = dk // 2
    col = lambda i, j: (i, j)
    return _call_with_cast_job(
        functools.partial(_qkvg_kernel, d=d, dk=dk, n_q=d // bn),
        cast=cast, cast_grid=(n_tiles, nj), step_of=lambda i, j: (i, j),
        grid=(n_tiles, nj),
        in_specs=[
            _resident((bm, d), lambda i, j: (0, 0)),
            pl.BlockSpec((rps, d), lambda i, j: (nxt(i) * nj + j, 0)),
            _resident((1, 1, 3 * d), lambda i, j: (0, 0, 0)),
            pl.BlockSpec((1, 1, 3 * d), lambda i, j: (nxt(i) // tps, 0, 0)),
            pl.BlockSpec((1, d), lambda i, j: (0, 0)),
            pl.BlockSpec((bm, half), lambda i, j: (i % tps, 0)),
            pl.BlockSpec((bm, half), lambda i, j: (i % tps, 0)),
            pl.BlockSpec((d, bn), lambda i, j: (0, j)),
            pl.BlockSpec((d, bn), lambda i, j: (0, nj + j)),
            pl.BlockSpec((d, bn), lambda i, j: (0, 2 * nj + j)),
        ],
        out_specs=[pl.BlockSpec((bm, bn), col), pl.BlockSpec((bm, bn), col), pl.BlockSpec((bm, bn), col)],
        out_shape=[
            jax.ShapeDtypeStruct((m, 2 * d), BF16),
            jax.ShapeDtypeStruct((m, vw), BF16),
            jax.ShapeDtypeStruct((m, vw), F32),
        ],
        scratch_shapes=[pltpu.VMEM((2, bm, d), BF16), pltpu.VMEM((bm, LANES), F32)],
        sem=("arbitrary", "arbitrary"),
        name="ret_qkvg_proj",
        args=(x, x, mod, mod, g_pre, cos, sin, w_in, w_in, w_in))


def _retention_kernel(q_ref, k_ref, v_ref, g_ref, gn_ref, dmat_ref, tab_ref, o_ref, state_ref, *, tr):
    tab = tab_ref[0]
    qdec, kdec, cdec = tab[:, 0:1], tab[:, 1:2], tab[0:1, 2:3]
    state_ref[...] = jnp.zeros_like(state_ref)

    for blk in range(q_ref.shape[0] // tr):
        rows = slice(blk * tr, (blk + 1) * tr)
        q = q_ref[rows, :]
        k = k_ref[rows, :]
        v = v_ref[rows, :]
        state = state_ref[...]

        scores = lax.dot_general(q, k, (((1,), (1,)), ((), ())), preferred_element_type=F32)
        p = (scores * dmat_ref[0]).astype(BF16)
        o = jnp.dot(p, v, preferred_element_type=F32)
        o = o + qdec * jnp.dot(q, state.astype(BF16), preferred_element_type=F32)

        kd = (k.astype(F32) * kdec).astype(BF16)
        state_ref[...] = state * cdec + lax.dot_general(
            kd, v, (((0,), (0,)), ((), ())), preferred_element_type=F32)

        mu = jnp.mean(o, axis=-1, keepdims=True)
        oc = o - mu
        var = jnp.mean(oc * oc, axis=-1, keepdims=True)
        on = oc * lax.rsqrt(var + EPS) * gn_ref[...]
        o_ref[rows, :] = (g_ref[rows, :] * on).astype(BF16)


def _retention_core(qk, v, g, gn_g, dmat, tab, b, s, t, cast=None):
    m = qk.shape[0]
    h = RET_HEADS
    tr = t["tr"]
    dk = qk.shape[1] // (2 * h)
    dv = v.shape[1] // h
    return _call_with_cast_job(
        functools.partial(_retention_kernel, tr=tr),
        cast=cast, cast_grid=(b * h, 1), step_of=lambda bi, hi: (bi * h + hi, 0),
        grid=(b, h),
        in_specs=[
            pl.BlockSpec((s, dk), lambda bi, hi: (bi, hi)),
            pl.BlockSpec((s, dk), lambda bi, hi: (bi, h + hi)),
            pl.BlockSpec((s, dv), lambda bi, hi: (bi, hi)),
            pl.BlockSpec((s, dv), lambda bi, hi: (bi, hi)),
            pl.BlockSpec((1, dv), lambda bi, hi: (0, hi)),
            pl.BlockSpec((1, tr, tr), lambda bi, hi: (hi, 0, 0)),
            pl.BlockSpec((1, tr, 128), lambda bi, hi: (hi, 0, 0)),
        ],
        out_specs=[pl.BlockSpec((s, dv), lambda bi, hi: (bi, hi))],
        out_shape=[jax.ShapeDtypeStruct((m, h * dv), BF16)],
        scratch_shapes=[pltpu.VMEM((dk, dv), F32)],
        sem=("parallel", "parallel"),
        name="ret_core",
        args=(qk, qk, v, g, gn_g, dmat, tab))


def _retention_tables(s, dk, tr):
    half = dk // 2
    inv = ROPE_BASE ** (-jnp.arange(half, dtype=F32) / half)
    ang = jnp.arange(s, dtype=F32)[:, None] * inv[None, :]
    gamma = 1.0 - jnp.exp2(-5.0 - jnp.arange(RET_HEADS, dtype=F32))
    log_g = jnp.log(gamma)
    r = jnp.arange(tr, dtype=F32)
    n, mm = r[:, None], r[None, :]
    cn, cm = jnp.floor(n / CHUNK), jnp.floor(mm / CHUNK)
    dmat = jnp.where(cm <= cn, jnp.exp(log_g[:, None, None] * jnp.abs(n - mm)), 0.0)
    tab = jnp.zeros((RET_HEADS, tr, 128), F32)
    tab = tab.at[:, :, 0].set(jnp.exp(log_g[:, None] * (r[None, :] + 1.0)))
    tab = tab.at[:, :, 1].set(jnp.exp(log_g[:, None] * (tr - 1.0 - r[None, :])))
    tab = tab.at[:, :, 2].set(jnp.exp(log_g * tr)[:, None])
    return jnp.cos(ang), jnp.sin(ang), dmat, tab


def _outproj_kernel(y_ref, x_ref, mod_ref, gpost_ref, w_ref, o_ref, inv_ref, *, res_w, d):
    o_ref[...] = jnp.dot(y_ref[...], w_ref[...], preferred_element_type=F32)
    _post_residual_rows(o_ref, x_ref, mod_ref, gpost_ref, o_ref, inv_ref, res_w, d)


def _outproj_sublayer(y, x, mod, g_post, w, res_w, s, t, cast=None):
    m, d = x.shape
    kk = y.shape[1]
    bm = t["bm"]
    tps = s // bm
    return _call_with_cast_job(
        functools.partial(_outproj_kernel, res_w=res_w, d=d),
        cast=cast, cast_grid=(m // bm, 1), step_of=lambda i: (i, 0),
        grid=(m // bm,),
        in_specs=[
            pl.BlockSpec((bm, kk), lambda i: (i, 0)),
            pl.BlockSpec((bm, d), lambda i: (i, 0)),
            pl.BlockSpec((1, 1, 3 * d), lambda i: (i // tps, 0, 0)),
            pl.BlockSpec((1, d), lambda i: (0, 0)),
            _resident((kk, d), lambda i: (0, 0)),
        ],
        out_specs=[pl.BlockSpec((bm, d), lambda i: (i, 0))],
        out_shape=[jax.ShapeDtypeStruct((m, d), F32)],
        scratch_shapes=[pltpu.VMEM((bm, LANES), F32)],
        sem=("parallel",),
        name="ret_outproj",
        args=(y, x, mod, g_post, w))


def _glu_kernel(x0_ref, xn_ref, mod0_ref, modn_ref, gpre_ref, w_ref, b_ref, u_ref, h_ref, inv_ref, *, d, bn):
    i = pl.program_id(0)
    slot = lax.rem(i, 2)

    @pl.when(i == 0)
    def _():
        _premod_rows(x0_ref, mod0_ref, gpre_ref, h_ref.at[0], inv_ref, d)

    h = h_ref[slot]
    for j in range(d // bn):
        ca, cb = slice(j * bn, (j + 1) * bn), slice(d + j * bn, d + (j + 1) * bn)
        a = jnp.dot(h, w_ref[:, ca], preferred_element_type=F32) + b_ref[:, ca]
        b = jnp.dot(h, w_ref[:, cb], preferred_element_type=F32) + b_ref[:, cb]
        u_ref[:, ca] = a * jax.nn.sigmoid(b)

    _premod_next(xn_ref, modn_ref, gpre_ref, h_ref, 1 - slot, 0, d)


def _glu_proj(x, mod, g_pre, w_in, b_in, s, t):
    m, d = x.shape
    bm, bn = t["bm"], t["bn"]
    tps = s // bm
    n_tiles = m // bm
    nxt = lambda i: jnp.minimum(i + 1, n_tiles - 1)
    return pl.pallas_call(
        functools.partial(_glu_kernel, d=d, bn=bn),
        grid=(n_tiles,),
        in_specs=[
            _resident((bm, d), lambda i: (0, 0)),
            pl.BlockSpec((bm, d), lambda i: (nxt(i), 0)),
            _resident((1, 1, 3 * d), lambda i: (0, 0, 0)),
            pl.BlockSpec((1, 1, 3 * d), lambda i: (nxt(i) // tps, 0, 0)),
            pl.BlockSpec((1, d), lambda i: (0, 0)),
            _resident((d, 2 * d), lambda i: (0, 0)),
            pl.BlockSpec((1, 2 * d), lambda i: (0, 0)),
        ],
        out_specs=pl.BlockSpec((bm, d), lambda i: (i, 0)),
        out_shape=jax.ShapeDtypeStruct((m, d), F32),
        scratch_shapes=[pltpu.VMEM((2, bm, d), BF16), pltpu.VMEM((bm, LANES), F32)],
        compiler_params=_params(("arbitrary",)),
        name="conv_glu_proj",
    )(x, x, mod, mod, g_pre, w_in, b_in)


def _conv_kernel(ucur_ref, uprev_ref, x_ref, mod_ref, dw_ref, dwb_ref, lng_ref, lnb_ref, w_ref,
                 bout_ref, gpost_ref, o_ref, ubuf_ref, cbuf_ref, hc_ref, z_ref, inv_ref,
                 *, res_w, d, tps, n_tiles, rc, cc):
    i = pl.program_id(0)
    bm = ucur_ref.shape[0]

    @pl.when(i == 0)
    def _():
        hc_ref[...] = jnp.zeros_like(hc_ref)

    first = (jnp.minimum(i, n_tiles - 1) % tps) == 0
    for c in range(d // cc):
        cols = slice(c * cc, (c + 1) * cc)
        ubuf_ref[c, 0:CONV_HALO, :] = jnp.where(first, 0.0, uprev_ref[:, cols])
        ubuf_ref[c, CONV_HALO:CONV_HALO + bm, :] = ucur_ref[:, cols]
    off = CONV_HALO - (CONV_WIDTH - 1)

    def blk_body(blk, carry):
        base = pl.multiple_of(blk * rc, rc)
        for c in range(d // cc):
            cols = slice(c * cc, (c + 1) * cc)
            acc = jnp.broadcast_to(dwb_ref[:, cols], (rc, cc))
            for tap in range(CONV_WIDTH):
                acc = acc + ubuf_ref[c, pl.ds(base + off + tap, rc), :] * dw_ref[tap:tap + 1, cols]
            cbuf_ref[pl.ds(base, rc), cols] = acc
        z_ref[blk] = jnp.dot(hc_ref[...], w_ref[blk], preferred_element_type=F32)
        return carry

    lax.fori_loop(0, bm // rc, blk_body, 0)

    @pl.when(i > 0)
    def _():
        _post_residual_slabs(z_ref, x_ref, mod_ref, gpost_ref, o_ref, inv_ref, bout_ref, res_w, d)

    cv = cbuf_ref[...]
    mu = jnp.mean(cv, axis=-1, keepdims=True)
    cz = cv - mu
    var = jnp.mean(cz * cz, axis=-1, keepdims=True)
    yn = cz * lax.rsqrt(var + EPS) * lng_ref[...] + lnb_ref[...]
    hc_ref[...] = _silu(yn).astype(BF16)


def _conv_sublayer(u, x, mod, dw, dw_b, ln_g, ln_b, w_out, b_out, g_post, r, res_w, s, t, cast=None):
    m, d = x.shape
    bm = t["bm"]
    tps = s // bm
    n_tiles = m // bm
    hb = bm // CONV_HALO
    rc, cc = min(CONV_ROW_CHUNK, bm), min(CONV_LANE_CHUNK, d)
    n_blk = bm // rc
    assert w_out.shape[1:] == (n_blk, d, d // n_blk)
    full = lambda i: (0, 0)
    cur = lambda i: jnp.minimum(i, n_tiles - 1)
    prev = lambda i: jnp.maximum(i - 1, 0)
    return _call_with_cast_job(
        functools.partial(_conv_kernel, res_w=res_w, d=d, tps=tps, n_tiles=n_tiles, rc=rc, cc=cc),
        cast=cast, cast_grid=(n_tiles, 1), step_of=lambda i: (cur(i), 0),
        grid=(n_tiles + 1,),
        in_specs=[
            pl.BlockSpec((bm, d), lambda i: (cur(i), 0)),
            pl.BlockSpec((CONV_HALO, d), lambda i: (jnp.maximum(cur(i) * hb - 1, 0), 0)),
            pl.BlockSpec((bm, d), lambda i: (prev(i), 0)),
            pl.BlockSpec((1, 1, 3 * d), lambda i: (prev(i) // tps, 0, 0)),
            pl.BlockSpec(dw.shape, full),
            pl.BlockSpec((1, d), full),
            pl.BlockSpec((1, d), full),
            pl.BlockSpec((1, d), full),
            _resident((None, n_blk, d, d // n_blk), lambda i: (r, 0, 0, 0)),
            pl.BlockSpec((1, d), full),
            pl.BlockSpec((1, d), full),
        ],
        out_specs=[pl.BlockSpec((bm, d), lambda i: (prev(i), 0))],
        out_shape=[jax.ShapeDtypeStruct((m, d), F32)],
        scratch_shapes=[
            pltpu.VMEM((d // cc, CONV_HALO + bm, cc), F32),
            pltpu.VMEM((bm, d), F32),
            pltpu.VMEM((bm, d), BF16),
            pltpu.VMEM((n_blk, bm, d // n_blk), F32),
            pltpu.VMEM((bm, LANES), F32),
        ],
        sem=("arbitrary",),
        name="conv_sublayer",
        args=(u, u, x, mod, dw, dw_b, ln_g, ln_b, w_out, b_out, g_post))


def kernel(x, c, ada_w, ada_b, norm_pre, norm_post, ffn_w_gate, ffn_w_up, ffn_w_down, ret_w_in, ret_gn,
           ret_w_out, conv_w_in, conv_b_in, conv_dw, conv_dw_b, conv_ln_g, conv_ln_b, conv_w_out, conv_b_out):
    b, s, d = x.shape
    depth = ada_w.shape[0]
    m = b * s
    f = ffn_w_gate.shape[-1]
    t = _tiles(m, s, d, f)
    assert s % t["bm"] == 0 and s % t["tr"] == 0 and t["tr"] % CHUNK == 0 and t["bm"] % CONV_HALO == 0
    assert CONV_HALO >= CONV_WIDTH - 1 and t["bn"] % t["dk"] == 0

    mods = _ada_mods(c, ada_w.reshape(depth * N_SUBLAYERS, d, 3 * d),
                     ada_b.reshape(depth * N_SUBLAYERS, 1, 3 * d), t)
    mods = mods.reshape(depth, N_SUBLAYERS, b, 1, 3 * d)
    row = lambda p: p.reshape(1, -1)
    cos, sin, dmat, tab = _retention_tables(s, t["dk"], t["tr"])
    dw_pad = CONV_HALO - CONV_WIDTH
    n_blk = t["bm"] // min(CONV_ROW_CHUNK, t["bm"])
    conv_out = conv_w_out.astype(BF16).reshape(-1, d, n_blk, d // n_blk).transpose(0, 2, 1, 3)

    stacked = dict(gate=ffn_w_gate, up=ffn_w_up, down=ffn_w_down, ret_in=ret_w_in, ret_out=ret_w_out,
                   conv_in=conv_w_in)
    ffn_names = ("gate", "up", "down")
    bf16_w = {}

    def weight(name, lead):
        if (name, lead) not in bf16_w:
            bf16_w[name, lead] = stacked[name][lead].astype(BF16)
        return bf16_w[name, lead]

    def jobs(keys):
        keys = [k for k in keys if k not in bf16_w and all(a < n for a, n in zip(k[1], stacked[k[0]].shape))]
        return keys, [(stacked[name], lead) for name, lead in keys]

    def store(keys, mats):
        bf16_w.update(zip(keys, mats))

    def ffn(xin, i, j, which, cast_keys):
        keys, cast = jobs(cast_keys)
        (out,), mats = _ffn_sublayer(
            xin, mods[i, j], row(norm_pre[i, j]), row(norm_post[i, j]),
            *[weight(n, (i, which)) for n in ffn_names], 0.5, s, t["bm"], t["bf"], cast=cast)
        store(keys, mats)
        return out

    ffn_keys = lambda i, which: [(n, (i, which)) for n in ffn_names]
    xf = x.reshape(m, d)
    for i in range(depth):
        r = i // 2
        if i == 0:
            first_jobs = [(n, (k,)) for n in ("ret_in", "ret_out", "conv_in") for k in range(stacked[n].shape[0])]
        else:
            first_jobs = []
        if i % 2 == 1:
            first_jobs += ffn_keys(i, 1)
        xf = ffn(xf, i, 0, 0, first_jobs)
        if i % 2 == 0:
            keys, cast = jobs(ffn_keys(i, 1))
            (qk, v, g), mats = _qkvg_proj(xf, mods[i, 1], row(norm_pre[i, 1]), cos, sin, weight("ret_in", (r,)),
                                          s, t, cast=cast)
            store(keys, mats)
            (y,), _ = _retention_core(qk, v, g, row(ret_gn[r]), dmat, tab, b, s, t)
            keys, cast = jobs(ffn_keys(i + 1, 0))
            (xf,), mats = _outproj_sublayer(y, xf, mods[i, 1], row(norm_post[i, 1]), weight("ret_out", (r,)),
                                            1.0, s, t, cast=cast)
            store(keys, mats)
        else:
            u = _glu_proj(xf, mods[i, 1], row(norm_pre[i, 1]), weight("conv_in", (r,)), row(conv_b_in[r]), s, t)
            (xf,), _ = _conv_sublayer(
                u, xf, mods[i, 1], jnp.pad(conv_dw[r], ((0, dw_pad), (0, 0))), row(conv_dw_b[r]),
                row(conv_ln_g[r]), row(conv_ln_b[r]), conv_out, row(conv_b_out[r]), row(norm_post[i, 1]),
                r, 1.0, s, t)
        xf = ffn(xf, i, 2, 1, [])
    return xf.reshape(b, s, d)
```

```python
import functools

import jax
import jax.numpy as jnp
from jax import lax
from jax.experimental import pallas as pl
from jax.experimental.pallas import tpu as pltpu

EPS = 1e-6
RET_HEADS = 8
CHUNK = 64
CONV_WIDTH = 31
ROPE_BASE = 10000.0
N_SUBLAYERS = 3

F32 = jnp.float32
BF16 = jnp.bfloat16

V7X_VMEM_LIMIT_BYTES = 60 * 1024 * 1024
LANES = 128
NORM_ROW_CHUNK = 16
CONV_HALO = 32
CONV_ROW_CHUNK = 64
CONV_LANE_CHUNK = LANES


def _tiles(m, s, d, f):
    dk = d // RET_HEADS
    return dict(
        bm=min(512, s),
        bm_qkvg=min(1024, s),
        bf=min(1024, f),
        bn=min(512, d),
        tr=min(256, s),
        ada_bn=min(1024, 3 * d),
        dk=dk,
    )


def _params(sem):
    return pltpu.CompilerParams(dimension_semantics=sem, vmem_limit_bytes=V7X_VMEM_LIMIT_BYTES)


def _resident(block_shape, index_map):
    return pl.BlockSpec(block_shape, index_map, pipeline_mode=pl.Buffered(1))


def _call_with_cast_job(body, *, grid, in_specs, out_specs, out_shape, scratch_shapes, sem, name, args,
                        cast=None, cast_grid=None, step_of=None):
    n_in, n_out = len(in_specs), len(out_specs)
    cast = cast or []
    in_specs, out_specs, out_shape, args = list(in_specs), list(out_specs), list(out_shape), list(args)
    for w, lead in cast:
        r, c = w.shape[-2:]
        assert r % cast_grid[0] == 0 and c % cast_grid[1] == 0 and len(lead) == w.ndim - 2
        rows, cols = r // cast_grid[0], c // cast_grid[1]
        in_specs.append(pl.BlockSpec((None,) * len(lead) + (rows, cols), lambda *g, lead=lead: lead + step_of(*g)))
        out_specs.append(pl.BlockSpec((rows, cols), lambda *g: step_of(*g)))
        out_shape.append(jax.ShapeDtypeStruct((r, c), BF16))
        args.append(w)
    n_cast = len(cast)

    def kernel(*refs):
        ins, cast_in = refs[:n_in], refs[n_in:n_in + n_cast]
        outs = refs[n_in + n_cast:n_in + n_cast + n_out]
        cast_out = refs[n_in + n_cast + n_out:n_in + 2 * n_cast + n_out]
        body(*ins, *outs, *refs[n_in + 2 * n_cast + n_out:])
        for src, dst in zip(cast_in, cast_out):
            dst[...] = src[...].astype(dst.dtype)

    res = pl.pallas_call(kernel, grid=grid, in_specs=in_specs, out_specs=out_specs, out_shape=out_shape,
                         scratch_shapes=scratch_shapes, compiler_params=_params(sem), name=name)(*args)
    return res[:n_out], list(res[n_out:])


def _row_loop(n_rows, body, unroll):
    rc = min(NORM_ROW_CHUNK, n_rows)
    n = n_rows // rc

    def step(r, carry):
        body(pl.ds(pl.multiple_of(r * rc, rc), rc))
        return carry

    lax.fori_loop(0, n, step, 0, unroll=min(unroll, n))


def _inv_rms_rows(z_ref, inv_ref, bias_ref=None):
    def body(rows):
        zv = z_ref[rows, :]
        if bias_ref is not None:
            zv = zv + bias_ref[...]
        inv = lax.rsqrt(jnp.mean(zv * zv, axis=-1, keepdims=True) + EPS)
        inv_ref[rows, :] = jnp.broadcast_to(inv, (zv.shape[0], inv_ref.shape[1]))

    _row_loop(z_ref.shape[0], body, unroll=8)


def _premod_rows(x_ref, mod_ref, gpre_ref, h_ref, inv_ref, d):
    _inv_rms_rows(x_ref, inv_ref)
    shift = mod_ref[0, :, 0:d]
    gain = gpre_ref[...] * (1.0 + mod_ref[0, :, d:2 * d])
    lanes = inv_ref.shape[1]

    def body(rows):
        inv = inv_ref[rows, :]
        for j in range(d // lanes):
            cs = slice(j * lanes, (j + 1) * lanes)
            h_ref[rows, cs] = (x_ref[rows, cs] * inv * gain[:, cs] + shift[:, cs]).astype(h_ref.dtype)

    _row_loop(x_ref.shape[0], body, unroll=4)


def _premod_next(xn_ref, modn_ref, gpre_ref, h_ref, slot, row0, d):
    n_rows = xn_ref.shape[0]
    rc = min(NORM_ROW_CHUNK, n_rows)
    shift = modn_ref[0, :, 0:d]
    gain = gpre_ref[...] * (1.0 + modn_ref[0, :, d:2 * d])
    for c in range(n_rows // rc):
        xv = xn_ref[c * rc:(c + 1) * rc, :]
        inv = lax.rsqrt(jnp.mean(xv * xv, axis=-1, keepdims=True) + EPS)
        start = row0 + c * rc
        rows = pl.ds(start if isinstance(start, int) else pl.multiple_of(start, rc), rc)
        h_ref[slot, rows, :] = (xv * inv * gain + shift).astype(h_ref.dtype)


def _post_residual_rows(z_ref, x_ref, mod_ref, gpost_ref, o_ref, inv_ref, res_w, d, bias_ref=None):
    _inv_rms_rows(z_ref, inv_ref, bias_ref)
    gain = (res_w * mod_ref[0, :, 2 * d:3 * d]) * gpost_ref[...]
    lanes = inv_ref.shape[1]

    def body(rows):
        inv = inv_ref[rows, :]
        for j in range(d // lanes):
            cs = slice(j * lanes, (j + 1) * lanes)
            zv = z_ref[rows, cs]
            if bias_ref is not None:
                zv = zv + bias_ref[:, cs]
            o_ref[rows, cs] = x_ref[rows, cs] + zv * inv * gain[:, cs]

    _row_loop(x_ref.shape[0], body, unroll=4)


def _post_residual_slabs(z_ref, x_ref, mod_ref, gpost_ref, o_ref, inv_ref, bias_ref, res_w, d):
    n_slabs, n_rows, zw = z_ref.shape
    lw = min(inv_ref.shape[1], zw)
    gain = (res_w * mod_ref[0, :, 2 * d:3 * d]) * gpost_ref[...]

    def sumsq_body(rows):
        ss = None
        for j in range(n_slabs):
            zv = z_ref[j, rows, :] + bias_ref[:, j * zw:(j + 1) * zw]
            part = jnp.sum(zv * zv, axis=-1, keepdims=True)
            ss = part if ss is None else ss + part
        inv = lax.rsqrt(ss * (1.0 / d) + EPS)
        inv_ref[rows, :] = jnp.broadcast_to(inv, (ss.shape[0], inv_ref.shape[1]))

    _row_loop(n_rows, sumsq_body, unroll=8)

    def out_body(rows):
        inv = inv_ref[rows, 0:lw]
        for j in range(n_slabs):
            for q in range(zw // lw):
                cs = slice(j * zw + q * lw, j * zw + (q + 1) * lw)
                zv = z_ref[j, rows, q * lw:(q + 1) * lw] + bias_ref[:, cs]
                o_ref[rows, cs] = x_ref[rows, cs] + zv * inv * gain[:, cs]

    _row_loop(n_rows, out_body, unroll=4)


def _silu(v):
    return v * jax.nn.sigmoid(v)


def _ada_kernel(c_ref, w_ref, b_ref, o_ref):
    cs = _silu(c_ref[...]).astype(BF16)
    o_ref[0] = jnp.dot(cs, w_ref[0].astype(BF16), preferred_element_type=F32) + b_ref[0]


def _ada_mods(c, ada_w, ada_b, t):
    n, d, e = ada_w.shape
    b = c.shape[0]
    bn = t["ada_bn"]
    return pl.pallas_call(
        _ada_kernel,
        grid=(n, e // bn),
        in_specs=[
            pl.BlockSpec((b, d), lambda i, j: (0, 0)),
            pl.BlockSpec((1, d, bn), lambda i, j: (i, 0, j)),
            pl.BlockSpec((1, 1, bn), lambda i, j: (i, 0, j)),
        ],
        out_specs=pl.BlockSpec((1, b, bn), lambda i, j: (i, 0, j)),
        out_shape=jax.ShapeDtypeStruct((n, b, e), F32),
        compiler_params=_params(("arbitrary", "arbitrary")),
        name="ada_mods",
    )(c, ada_w, ada_b)


def _ffn_kernel(x_ref, xn_ref, mod_ref, modn_ref, gpre_ref, gpost_ref, wg_ref, wu_ref, wd_ref, o_ref,
                h_ref, inv_ref, *, res_w, d):
    i = pl.program_id(0)
    f = pl.program_id(1)
    slot = lax.rem(i, 2)

    @pl.when(jnp.logical_and(i == 0, f == 0))
    def _():
        _premod_rows(x_ref, mod_ref, gpre_ref, h_ref.at[0], inv_ref, d)

    @pl.when(f == 0)
    def _():
        o_ref[...] = jnp.zeros_like(o_ref)

    h = h_ref[slot]
    hg = jnp.dot(h, wg_ref[...], preferred_element_type=F32)
    hu = jnp.dot(h, wu_ref[...], preferred_element_type=F32)
    a = (_silu(hg) * hu).astype(BF16)
    _premod_next(xn_ref, modn_ref, gpre_ref, h_ref, 1 - slot, f * xn_ref.shape[0], d)
    o_ref[...] += jnp.dot(a, wd_ref[...], preferred_element_type=F32)

    @pl.when(f == pl.num_programs(1) - 1)
    def _():
        _post_residual_rows(o_ref, x_ref, mod_ref, gpost_ref, o_ref, inv_ref, res_w, d)


def _ffn_sublayer(x, mod, g_pre, g_post, wg, wu, wd, res_w, s, bm, bf, cast=None):
    m, d = x.shape
    f = wg.shape[-1]
    tps = s // bm
    n_tiles, nf = m // bm, f // bf
    rps = bm // nf
    assert bm % nf == 0 and rps % min(NORM_ROW_CHUNK, rps) == 0
    nxt = lambda i: jnp.minimum(i + 1, n_tiles - 1)
    return _call_with_cast_job(
        functools.partial(_ffn_kernel, res_w=res_w, d=d),
        cast=cast, cast_grid=(n_tiles, nf), step_of=lambda i, j: (i, j),
        grid=(n_tiles, nf),
        in_specs=[
            pl.BlockSpec((bm, d), lambda i, j: (i, 0)),
            pl.BlockSpec((rps, d), lambda i, j: (nxt(i) * nf + j, 0)),
            pl.BlockSpec((1, 1, 3 * d), lambda i, j: (i // tps, 0, 0)),
            pl.BlockSpec((1, 1, 3 * d), lambda i, j: (nxt(i) // tps, 0, 0)),
            pl.BlockSpec((1, d), lambda i, j: (0, 0)),
            pl.BlockSpec((1, d), lambda i, j: (0, 0)),
            pl.BlockSpec((d, bf), lambda i, j: (0, j)),
            pl.BlockSpec((d, bf), lambda i, j: (0, j)),
            pl.BlockSpec((bf, d), lambda i, j: (j, 0)),
        ],
        out_specs=[pl.BlockSpec((bm, d), lambda i, j: (i, 0))],
        out_shape=[jax.ShapeDtypeStruct((m, d), F32)],
        scratch_shapes=[pltpu.VMEM((2, bm, d), BF16), pltpu.VMEM((bm, LANES), F32)],
        sem=("arbitrary", "arbitrary"),
        name="ffn_sublayer",
        args=(x, x, mod, mod, g_pre, g_post, wg, wu, wd))


def _qkvg_kernel(x0_ref, xn_ref, mod0_ref, modn_ref, gpre_ref, cos_ref, sin_ref, wqk_ref, wv_ref, wg_ref,
                 qk_ref, v_ref, g_ref, h_ref, inv_ref, *, d, dk, n_q):
    i = pl.program_id(0)
    j = pl.program_id(1)
    slot = lax.rem(i, 2)

    @pl.when(jnp.logical_and(i == 0, j == 0))
    def _():
        _premod_rows(x0_ref, mod0_ref, gpre_ref, h_ref.at[0], inv_ref, d)

    h = h_ref[slot]
    g_ref[...] = _silu(jnp.dot(h, wg_ref[...], preferred_element_type=F32)).astype(BF16)
    y = jnp.dot(h, wqk_ref[...], preferred_element_type=F32)
    _premod_next(xn_ref, modn_ref, gpre_ref, h_ref, 1 - slot, j * xn_ref.shape[0], d)
    v_ref[...] = jnp.dot(h, wv_ref[...], preferred_element_type=F32).astype(BF16)

    half = dk // 2
    sc = jnp.where(j >= n_q, dk ** -0.5, 1.0).astype(F32)
    cos = cos_ref[...] * sc
    sin = sin_ref[...] * sc
    for hh in range(y.shape[1] // dk):
        t1 = y[:, hh * dk:hh * dk + half]
        t2 = y[:, hh * dk + half:(hh + 1) * dk]
        qk_ref[:, hh * dk:hh * dk + half] = (t1 * cos - t2 * sin).astype(BF16)
        qk_ref[:, hh * dk + half:(hh + 1) * dk] = (t1 * sin + t2 * cos).astype(BF16)


def _qkvg_proj(x, mod, g_pre, cos, sin, w_in, s, t, cast=None):
    m, d = x.shape
    bm, bn, dk = t["bm_qkvg"], t["bn"], t["dk"]
    tps = s // bm
    vw = 2 * d
    nj = vw // bn
    n_tiles = m // bm
    rps = bm // nj
    assert bm % nj == 0 and rps % min(NORM_ROW_CHUNK, rps) == 0
    nxt = lambda i: jnp.minimum(i + 1, n_tiles - 1)
    half = dk // 2
    col = lambda i, j: (i, j)
    return _call_with_cast_job(
        functools.partial(_qkvg_kernel, d=d, dk=dk, n_q=d // bn),
        cast=cast, cast_grid=(n_tiles, nj), step_of=lambda i, j: (i, j),
        grid=(n_tiles, nj),
        in_specs=[
            _resident((bm, d), lambda i, j: (0, 0)),
            pl.BlockSpec((rps, d), lambda i, j: (nxt(i) * nj + j, 0)),
            _resident((1, 1, 3 * d), lambda i, j: (0, 0, 0)),
            pl.BlockSpec((1, 1, 3 * d), lambda i, j: (nxt(i) // tps, 0, 0)),
            pl.BlockSpec((1, d), lambda i, j: (0, 0)),
            pl.BlockSpec((bm, half), lambda i, j: (i % tps, 0)),
            pl.BlockSpec((bm, half), lambda i, j: (i % tps, 0)),
            pl.BlockSpec((d, bn), lambda i, j: (0, j)),
            pl.BlockSpec((d, bn), lambda i, j: (0, nj + j)),
            pl.BlockSpec((d, bn), lambda i, j: (0, 2 * nj + j)),
        ],
        out_specs=[pl.BlockSpec((bm, bn), col), pl.BlockSpec((bm, bn), col), pl.BlockSpec((bm, bn), col)],
        out_shape=[
            jax.ShapeDtypeStruct((m, 2 * d), BF16),
            jax.ShapeDtypeStruct((m, vw), BF16),
            jax.ShapeDtypeStruct((m, vw), BF16),
        ],
        scratch_shapes=[pltpu.VMEM((2, bm, d), BF16), pltpu.VMEM((bm, LANES), F32)],
        sem=("arbitrary", "arbitrary"),
        name="ret_qkvg_proj",
        args=(x, x, mod, mod, g_pre, cos, sin, w_in, w_in, w_in))


def _retention_kernel(q_ref, k_ref, v_ref, g_ref, gn_ref, dmat_ref, tab_ref, o_ref, state_ref, *, tr):
    tab = tab_ref[0]
    qdec, kdec, cdec = tab[:, 0:1], tab[:, 1:2], tab[0:1, 2:3]
    state_ref[...] = jnp.zeros_like(state_ref)

    for blk in range(q_ref.shape[0] // tr):
        rows = slice(blk * tr, (blk + 1) * tr)
        q = q_ref[rows, :]
        k = k_ref[rows, :]
        v = v_ref[rows, :]
        state = state_ref[...]

        scores = lax.dot_general(q, k, (((1,), (1,)), ((), ())), preferred_element_type=F32)
        p = (scores * dmat_ref[0]).astype(BF16)
        o = jnp.dot(p, v, preferred_element_type=F32)
        o = o + qdec * jnp.dot(q, state.astype(BF16), preferred_element_type=F32)

        kd = (k.astype(F32) * kdec).astype(BF16)
        state_ref[...] = state * cdec + lax.dot_general(
            kd, v, (((0,), (0,)), ((), ())), preferred_element_type=F32)

        mu = jnp.mean(o, axis=-1, keepdims=True)
        oc = o - mu
        var = jnp.mean(oc * oc, axis=-1, keepdims=True)
        on = oc * lax.rsqrt(var + EPS) * gn_ref[...]
        o_ref[rows, :] = (g_ref[rows, :].astype(F32) * on).astype(BF16)


def _retention_core(qk, v, g, gn_g, dmat, tab, b, s, t, cast=None):
    m = qk.shape[0]
    h = RET_HEADS
    tr = t["tr"]
    dk = qk.shape[1] // (2 * h)
    dv = v.shape[1] // h
    return _call_with_cast_job(
        functools.partial(_retention_kernel, tr=tr),
        cast=cast, cast_grid=(b * h, 1), step_of=lambda bi, hi: (bi * h + hi, 0),
        grid=(b, h),
        in_specs=[
            pl.BlockSpec((s, dk), lambda bi, hi: (bi, hi)),
            pl.BlockSpec((s, dk), lambda bi, hi: (bi, h + hi)),
            pl.BlockSpec((s, dv), lambda bi, hi: (bi, hi)),
            pl.BlockSpec((s, dv), lambda bi, hi: (bi, hi)),
            pl.BlockSpec((1, dv), lambda bi, hi: (0, hi)),
            pl.BlockSpec((1, tr, tr), lambda bi, hi: (hi, 0, 0)),
            pl.BlockSpec((1, tr, 128), lambda bi, hi: (hi, 0, 0)),
        ],
        out_specs=[pl.BlockSpec((s, dv), lambda bi, hi: (bi, hi))],
        out_shape=[jax.ShapeDtypeStruct((m, h * dv), BF16)],
        scratch_shapes=[pltpu.VMEM((dk, dv), F32)],
        sem=("parallel", "parallel"),
        name="ret_core",
        args=(qk, qk, v, g, gn_g, dmat, tab))


def _retention_tables(s, dk, tr):
    half = dk // 2
    inv = ROPE_BASE ** (-jnp.arange(half, dtype=F32) / half)
    ang = jnp.arange(s, dtype=F32)[:, None] * inv[None, :]
    gamma = 1.0 - jnp.exp2(-5.0 - jnp.arange(RET_HEADS, dtype=F32))
    log_g = jnp.log(gamma)
    r = jnp.arange(tr, dtype=F32)
    n, mm = r[:, None], r[None, :]
    cn, cm = jnp.floor(n / CHUNK), jnp.floor(mm / CHUNK)
    dmat = jnp.where(cm <= cn, jnp.exp(log_g[:, None, None] * jnp.abs(n - mm)), 0.0)
    tab = jnp.zeros((RET_HEADS, tr, 128), F32)
    tab = tab.at[:, :, 0].set(jnp.exp(log_g[:, None] * (r[None, :] + 1.0)))
    tab = tab.at[:, :, 1].set(jnp.exp(log_g[:, None] * (tr - 1.0 - r[None, :])))
    tab = tab.at[:, :, 2].set(jnp.exp(log_g * tr)[:, None])
    return jnp.cos(ang), jnp.sin(ang), dmat, tab


def _outproj_kernel(y_ref, x_ref, mod_ref, gpost_ref, w_ref, o_ref, inv_ref, *, res_w, d):
    o_ref[...] = jnp.dot(y_ref[...], w_ref[...], preferred_element_type=F32)
    _post_residual_rows(o_ref, x_ref, mod_ref, gpost_ref, o_ref, inv_ref, res_w, d)


def _outproj_sublayer(y, x, mod, g_post, w, res_w, s, t, cast=None):
    m, d = x.shape
    kk = y.shape[1]
    bm = t["bm"]
    tps = s // bm
    return _call_with_cast_job(
        functools.partial(_outproj_kernel, res_w=res_w, d=d),
        cast=cast, cast_grid=(m // bm, 1), step_of=lambda i: (i, 0),
        grid=(m // bm,),
        in_specs=[
            pl.BlockSpec((bm, kk), lambda i: (i, 0)),
            pl.BlockSpec((bm, d), lambda i: (i, 0)),
            pl.BlockSpec((1, 1, 3 * d), lambda i: (i // tps, 0, 0)),
            pl.BlockSpec((1, d), lambda i: (0, 0)),
            _resident((kk, d), lambda i: (0, 0)),
        ],
        out_specs=[pl.BlockSpec((bm, d), lambda i: (i, 0))],
        out_shape=[jax.ShapeDtypeStruct((m, d), F32)],
        scratch_shapes=[pltpu.VMEM((bm, LANES), F32)],
        sem=("parallel",),
        name="ret_outproj",
        args=(y, x, mod, g_post, w))


def _glu_kernel(x0_ref, xn_ref, mod0_ref, modn_ref, gpre_ref, w_ref, b_ref, u_ref, h_ref, inv_ref, *, d, bn):
    i = pl.program_id(0)
    slot = lax.rem(i, 2)

    @pl.when(i == 0)
    def _():
        _premod_rows(x0_ref, mod0_ref, gpre_ref, h_ref.at[0], inv_ref, d)

    h = h_ref[slot]
    for j in range(d // bn):
        ca, cb = slice(j * bn, (j + 1) * bn), slice(d + j * bn, d + (j + 1) * bn)
        a = jnp.dot(h, w_ref[:, ca], preferred_element_type=F32) + b_ref[:, ca]
        b = jnp.dot(h, w_ref[:, cb], preferred_element_type=F32) + b_ref[:, cb]
        u_ref[:, ca] = a * jax.nn.sigmoid(b)

    _premod_next(xn_ref, modn_ref, gpre_ref, h_ref, 1 - slot, 0, d)


def _glu_proj(x, mod, g_pre, w_in, b_in, s, t):
    m, d = x.shape
    bm, bn = t["bm"], t["bn"]
    tps = s // bm
    n_tiles = m // bm
    nxt = lambda i: jnp.minimum(i + 1, n_tiles - 1)
    return pl.pallas_call(
        functools.partial(_glu_kernel, d=d, bn=bn),
        grid=(n_tiles,),
        in_specs=[
            _resident((bm, d), lambda i: (0, 0)),
            pl.BlockSpec((bm, d), lambda i: (nxt(i), 0)),
            _resident((1, 1, 3 * d), lambda i: (0, 0, 0)),
            pl.BlockSpec((1, 1, 3 * d), lambda i: (nxt(i) // tps, 0, 0)),
            pl.BlockSpec((1, d), lambda i: (0, 0)),
            _resident((d, 2 * d), lambda i: (0, 0)),
            pl.BlockSpec((1, 2 * d), lambda i: (0, 0)),
        ],
        out_specs=pl.BlockSpec((bm, d), lambda i: (i, 0)),
        out_shape=jax.ShapeDtypeStruct((m, d), F32),
        scratch_shapes=[pltpu.VMEM((2, bm, d), BF16), pltpu.VMEM((bm, LANES), F32)],
        compiler_params=_params(("arbitrary",)),
        name="conv_glu_proj",
    )(x, x, mod, mod, g_pre, w_in, b_in)


def _conv_kernel(ucur_ref, uprev_ref, x_ref, mod_ref, dw_ref, dwb_ref, lng_ref, lnb_ref, w_ref,
                 bout_ref, gpost_ref, o_ref, ubuf_ref, cbuf_ref, hc_ref, z_ref, inv_ref,
                 *, res_w, d, tps, n_tiles, rc, cc):
    i = pl.program_id(0)
    bm = ucur_ref.shape[0]

    @pl.when(i == 0)
    def _():
        hc_ref[...] = jnp.zeros_like(hc_ref)

    first = (jnp.minimum(i, n_tiles - 1) % tps) == 0
    for c in range(d // cc):
        cols = slice(c * cc, (c + 1) * cc)
        ubuf_ref[c, 0:CONV_HALO, :] = jnp.where(first, 0.0, uprev_ref[:, cols])
        ubuf_ref[c, CONV_HALO:CONV_HALO + bm, :] = ucur_ref[:, cols]
    off = CONV_HALO - (CONV_WIDTH - 1)

    def blk_body(blk, carry):
        base = pl.multiple_of(blk * rc, rc)
        for c in range(d // cc):
            cols = slice(c * cc, (c + 1) * cc)
            acc = jnp.broadcast_to(dwb_ref[:, cols], (rc, cc))
            for tap in range(CONV_WIDTH):
                acc = acc + ubuf_ref[c, pl.ds(base + off + tap, rc), :] * dw_ref[tap:tap + 1, cols]
            cbuf_ref[pl.ds(base, rc), cols] = acc
        z_ref[blk] = jnp.dot(hc_ref[...], w_ref[blk], preferred_element_type=F32)
        return carry

    lax.fori_loop(0, bm // rc, blk_body, 0)

    @pl.when(i > 0)
    def _():
        _post_residual_slabs(z_ref, x_ref, mod_ref, gpost_ref, o_ref, inv_ref, bout_ref, res_w, d)

    cv = cbuf_ref[...]
    mu = jnp.mean(cv, axis=-1, keepdims=True)
    cz = cv - mu
    var = jnp.mean(cz * cz, axis=-1, keepdims=True)
    yn = cz * lax.rsqrt(var + EPS) * lng_ref[...] + lnb_ref[...]
    hc_ref[...] = _silu(yn).astype(BF16)


def _conv_sublayer(u, x, mod, dw, dw_b, ln_g, ln_b, w_out, b_out, g_post, r, res_w, s, t, cast=None):
    m, d = x.shape
    bm = t["bm"]
    tps = s // bm
    n_tiles = m // bm
    hb = bm // CONV_HALO
    rc, cc = min(CONV_ROW_CHUNK, bm), min(CONV_LANE_CHUNK, d)
    n_blk = bm // rc
    assert w_out.shape[1:] == (n_blk, d, d // n_blk)
    full = lambda i: (0, 0)
    cur = lambda i: jnp.minimum(i, n_tiles - 1)
    prev = lambda i: jnp.maximum(i - 1, 0)
    return _call_with_cast_job(
        functools.partial(_conv_kernel, res_w=res_w, d=d, tps=tps, n_tiles=n_tiles, rc=rc, cc=cc),
        cast=cast, cast_grid=(n_tiles, 1), step_of=lambda i: (cur(i), 0),
        grid=(n_tiles + 1,),
        in_specs=[
            pl.BlockSpec((bm, d), lambda i: (cur(i), 0)),
            pl.BlockSpec((CONV_HALO, d), lambda i: (jnp.maximum(cur(i) * hb - 1, 0), 0)),
            pl.BlockSpec((bm, d), lambda i: (prev(i), 0)),
            pl.BlockSpec((1, 1, 3 * d), lambda i: (prev(i) // tps, 0, 0)),
            pl.BlockSpec(dw.shape, full),
            pl.BlockSpec((1, d), full),
            pl.BlockSpec((1, d), full),
            pl.BlockSpec((1, d), full),
            _resident((None, n_blk, d, d // n_blk), lambda i: (r, 0, 0, 0)),
            pl.BlockSpec((1, d), full),
            pl.BlockSpec((1, d), full),
        ],
        out_specs=[pl.BlockSpec((bm, d), lambda i: (prev(i), 0))],
        out_shape=[jax.ShapeDtypeStruct((m, d), F32)],
        scratch_shapes=[
            pltpu.VMEM((d // cc, CONV_HALO + bm, cc), F32),
            pltpu.VMEM((bm, d), F32),
            pltpu.VMEM((bm, d), BF16),
            pltpu.VMEM((n_blk, bm, d // n_blk), F32),
            pltpu.VMEM((bm, LANES), F32),
        ],
        sem=("arbitrary",),
        name="conv_sublayer",
        args=(u, u, x, mod, dw, dw_b, ln_g, ln_b, w_out, b_out, g_post))


def kernel(x, c, ada_w, ada_b, norm_pre, norm_post, ffn_w_gate, ffn_w_up, ffn_w_down, ret_w_in, ret_gn,
           ret_w_out, conv_w_in, conv_b_in, conv_dw, conv_dw_b, conv_ln_g, conv_ln_b, conv_w_out, conv_b_out):
    b, s, d = x.shape
    depth = ada_w.shape[0]
    m = b * s
    f = ffn_w_gate.shape[-1]
    t = _tiles(m, s, d, f)
    assert s % t["bm"] == 0 and s % t["tr"] == 0 and t["tr"] % CHUNK == 0 and t["bm"] % CONV_HALO == 0
    assert CONV_HALO >= CONV_WIDTH - 1 and t["bn"] % t["dk"] == 0

    mods = _ada_mods(c, ada_w.reshape(depth * N_SUBLAYERS, d, 3 * d),
                     ada_b.reshape(depth * N_SUBLAYERS, 1, 3 * d), t)
    mods = mods.reshape(depth, N_SUBLAYERS, b, 1, 3 * d)
    row = lambda p: p.reshape(1, -1)
    cos, sin, dmat, tab = _retention_tables(s, t["dk"], t["tr"])
    dw_pad = CONV_HALO - CONV_WIDTH
    n_blk = t["bm"] // min(CONV_ROW_CHUNK, t["bm"])
    conv_out = conv_w_out.astype(BF16).reshape(-1, d, n_blk, d // n_blk).transpose(0, 2, 1, 3)

    stacked = dict(gate=ffn_w_gate, up=ffn_w_up, down=ffn_w_down, ret_in=ret_w_in, ret_out=ret_w_out,
                   conv_in=conv_w_in)
    ffn_names = ("gate", "up", "down")
    bf16_w = {}

    def weight(name, lead):
        if (name, lead) not in bf16_w:
            bf16_w[name, lead] = stacked[name][lead].astype(BF16)
        return bf16_w[name, lead]

    def jobs(keys):
        keys = [k for k in keys if k not in bf16_w and all(a < n for a, n in zip(k[1], stacked[k[0]].shape))]
        return keys, [(stacked[name], lead) for name, lead in keys]

    def store(keys, mats):
        bf16_w.update(zip(keys, mats))

    def ffn(xin, i, j, which, cast_keys):
        keys, cast = jobs(cast_keys)
        (out,), mats = _ffn_sublayer(
            xin, mods[i, j], row(norm_pre[i, j]), row(norm_post[i, j]),
            *[weight(n, (i, which)) for n in ffn_names], 0.5, s, t["bm"], t["bf"], cast=cast)
        store(keys, mats)
        return out

    ffn_keys = lambda i, which: [(n, (i, which)) for n in ffn_names]
    xf = x.reshape(m, d)
    for i in range(depth):
        r = i // 2
        if i == 0:
            first_jobs = [(n, (k,)) for n in ("ret_in", "ret_out", "conv_in") for k in range(stacked[n].shape[0])]
        else:
            first_jobs = []
        if i % 2 == 1:
            first_jobs += ffn_keys(i, 1)
        xf = ffn(xf, i, 0, 0, first_jobs)
        if i % 2 == 0:
            keys, cast = jobs(ffn_keys(i, 1))
            (qk, v, g), mats = _qkvg_proj(xf, mods[i, 1], row(norm_pre[i, 1]), cos, sin, weight("ret_in", (r,)),
                                          s, t, cast=cast)
            store(keys, mats)
            (y,), _ = _retention_core(qk, v, g, row(ret_gn[r]), dmat, tab, b, s, t)
            keys, cast = jobs(ffn_keys(i + 1, 0))
            (xf,), mats = _outproj_sublayer(y, xf, mods[i, 1], row(norm_post[i, 1]), weight("ret_out", (r,)),
                                            1.0, s, t, cast=cast)
            store(keys, mats)
        else:
            u = _glu_proj(xf, mods[i, 1], row(norm_pre[i, 1]), weight("conv_in", (r,)), row(conv_b_in[r]), s, t)
            (xf,), _ = _conv_sublayer(
                u, xf, mods[i, 1], jnp.pad(conv_dw[r], ((0, dw_pad), (0, 0))), row(conv_dw_b[r]),
                row(conv_ln_g[r]), row(conv_ln_b[r]), conv_out, row(conv_b_out[r]), row(norm_post[i, 1]),
                r, 1.0, s, t)
        xf = ffn(xf, i, 2, 1, [])
    return xf.reshape(b, s, d)
```

```python
import functools

import jax
import jax.numpy as jnp
from jax import lax
from jax.experimental import pallas as pl
from jax.experimental.pallas import tpu as pltpu

EPS = 1e-6
RET_HEADS = 8
CHUNK = 64
CONV_WIDTH = 31
ROPE_BASE = 10000.0
N_SUBLAYERS = 3

F32 = jnp.float32
BF16 = jnp.bfloat16

V7X_VMEM_LIMIT_BYTES = 60 * 1024 * 1024
LANES = 128
NORM_ROW_CHUNK = 16
CONV_HALO = 32
CONV_ROW_CHUNK = 64
CONV_LANE_CHUNK = LANES


def _tiles(m, s, d, f):
    dk = d // RET_HEADS
    return dict(
        bm=min(512, s),
        bm_qkvg=min(1024, s),
        bf=min(1024, f),
        bn=min(512, d),
        tr=min(256, s),
        ada_bn=min(1024, 3 * d),
        dk=dk,
    )


def _params(sem):
    return pltpu.CompilerParams(dimension_semantics=sem, vmem_limit_bytes=V7X_VMEM_LIMIT_BYTES)


def _resident(block_shape, index_map):
    return pl.BlockSpec(block_shape, index_map, pipeline_mode=pl.Buffered(1))


def _call_with_cast_job(body, *, grid, in_specs, out_specs, out_shape, scratch_shapes, sem, name, args,
                        cast=None, cast_grid=None, step_of=None):
    n_in, n_out = len(in_specs), len(out_specs)
    cast = cast or []
    in_specs, out_specs, out_shape, args = list(in_specs), list(out_specs), list(out_shape), list(args)
    for w, lead, col_tiled in cast:
        r, c = w.shape[-2:]
        assert r % cast_grid[0] == 0 and c % cast_grid[1] == 0 and len(lead) == w.ndim - 2
        rows, cols = r // cast_grid[0], c // cast_grid[1]
        in_specs.append(pl.BlockSpec((None,) * len(lead) + (rows, cols), lambda *g, lead=lead: lead + step_of(*g)))
        if col_tiled:
            out_specs.append(pl.BlockSpec((None, rows, cols), lambda *g: step_of(*g)[::-1] + (0,)))
            out_shape.append(jax.ShapeDtypeStruct((cast_grid[1], r, cols), BF16))
        else:
            out_specs.append(pl.BlockSpec((rows, cols), lambda *g: step_of(*g)))
            out_shape.append(jax.ShapeDtypeStruct((r, c), BF16))
        args.append(w)
    n_cast = len(cast)

    def kernel(*refs):
        ins, cast_in = refs[:n_in], refs[n_in:n_in + n_cast]
        outs = refs[n_in + n_cast:n_in + n_cast + n_out]
        cast_out = refs[n_in + n_cast + n_out:n_in + 2 * n_cast + n_out]
        body(*ins, *outs, *refs[n_in + 2 * n_cast + n_out:])
        for src, dst in zip(cast_in, cast_out):
            dst[...] = src[...].astype(dst.dtype)

    res = pl.pallas_call(kernel, grid=grid, in_specs=in_specs, out_specs=out_specs, out_shape=out_shape,
                         scratch_shapes=scratch_shapes, compiler_params=_params(sem), name=name)(*args)
    return res[:n_out], list(res[n_out:])


def _row_loop(n_rows, body, unroll):
    rc = min(NORM_ROW_CHUNK, n_rows)
    n = n_rows // rc

    def step(r, carry):
        body(pl.ds(pl.multiple_of(r * rc, rc), rc))
        return carry

    lax.fori_loop(0, n, step, 0, unroll=min(unroll, n))


def _inv_rms_rows(z_ref, inv_ref, bias_ref=None):
    def body(rows):
        zv = z_ref[rows, :]
        if bias_ref is not None:
            zv = zv + bias_ref[...]
        inv = lax.rsqrt(jnp.mean(zv * zv, axis=-1, keepdims=True) + EPS)
        inv_ref[rows, :] = jnp.broadcast_to(inv, (zv.shape[0], inv_ref.shape[1]))

    _row_loop(z_ref.shape[0], body, unroll=8)


def _premod_rows(x_ref, mod_ref, gpre_ref, h_ref, inv_ref, d):
    _inv_rms_rows(x_ref, inv_ref)
    shift = mod_ref[0, :, 0:d]
    gain = gpre_ref[...] * (1.0 + mod_ref[0, :, d:2 * d])
    lanes = inv_ref.shape[1]

    def body(rows):
        inv = inv_ref[rows, :]
        for j in range(d // lanes):
            cs = slice(j * lanes, (j + 1) * lanes)
            h_ref[rows, cs] = (x_ref[rows, cs] * inv * gain[:, cs] + shift[:, cs]).astype(h_ref.dtype)

    _row_loop(x_ref.shape[0], body, unroll=4)


def _premod_next(xn_ref, modn_ref, gpre_ref, h_ref, slot, row0, d):
    n_rows = xn_ref.shape[0]
    rc = min(NORM_ROW_CHUNK, n_rows)
    shift = modn_ref[0, :, 0:d]
    gain = gpre_ref[...] * (1.0 + modn_ref[0, :, d:2 * d])
    for c in range(n_rows // rc):
        xv = xn_ref[c * rc:(c + 1) * rc, :]
        inv = lax.rsqrt(jnp.mean(xv * xv, axis=-1, keepdims=True) + EPS)
        start = row0 + c * rc
        rows = pl.ds(start if isinstance(start, int) else pl.multiple_of(start, rc), rc)
        h_ref[slot, rows, :] = (xv * inv * gain + shift).astype(h_ref.dtype)


def _post_residual_rows(z_ref, x_ref, mod_ref, gpost_ref, o_ref, inv_ref, res_w, d, bias_ref=None):
    _inv_rms_rows(z_ref, inv_ref, bias_ref)
    gain = (res_w * mod_ref[0, :, 2 * d:3 * d]) * gpost_ref[...]
    lanes = inv_ref.shape[1]

    def body(rows):
        inv = inv_ref[rows, :]
        for j in range(d // lanes):
            cs = slice(j * lanes, (j + 1) * lanes)
            zv = z_ref[rows, cs]
            if bias_ref is not None:
                zv = zv + bias_ref[:, cs]
            o_ref[rows, cs] = x_ref[rows, cs] + zv * inv * gain[:, cs]

    _row_loop(x_ref.shape[0], body, unroll=4)


def _post_residual_slabs(z_ref, x_ref, mod_ref, gpost_ref, o_ref, inv_ref, bias_ref, res_w, d):
    n_slabs, n_rows, zw = z_ref.shape
    lw = min(inv_ref.shape[1], zw)
    gain = (res_w * mod_ref[0, :, 2 * d:3 * d]) * gpost_ref[...]

    def sumsq_body(rows):
        ss = None
        for j in range(n_slabs):
            zv = z_ref[j, rows, :] + bias_ref[:, j * zw:(j + 1) * zw]
            part = jnp.sum(zv * zv, axis=-1, keepdims=True)
            ss = part if ss is None else ss + part
        inv = lax.rsqrt(ss * (1.0 / d) + EPS)
        inv_ref[rows, :] = jnp.broadcast_to(inv, (ss.shape[0], inv_ref.shape[1]))

    _row_loop(n_rows, sumsq_body, unroll=8)

    def out_body(rows):
        inv = inv_ref[rows, 0:lw]
        for j in range(n_slabs):
            for q in range(zw // lw):
                cs = slice(j * zw + q * lw, j * zw + (q + 1) * lw)
                zv = z_ref[j, rows, q * lw:(q + 1) * lw] + bias_ref[:, cs]
                o_ref[rows, cs] = x_ref[rows, cs] + zv * inv * gain[:, cs]

    _row_loop(n_rows, out_body, unroll=4)


def _silu(v):
    return v * jax.nn.sigmoid(v)


def _ada_kernel(c_ref, w_ref, b_ref, o_ref):
    cs = _silu(c_ref[...]).astype(BF16)
    o_ref[0] = jnp.dot(cs, w_ref[0].astype(BF16), preferred_element_type=F32) + b_ref[0]


def _ada_mods(c, ada_w, ada_b, t):
    n, d, e = ada_w.shape
    b = c.shape[0]
    bn = t["ada_bn"]
    return pl.pallas_call(
        _ada_kernel,
        grid=(n, e // bn),
        in_specs=[
            pl.BlockSpec((b, d), lambda i, j: (0, 0)),
            pl.BlockSpec((1, d, bn), lambda i, j: (i, 0, j)),
            pl.BlockSpec((1, 1, bn), lambda i, j: (i, 0, j)),
        ],
        out_specs=pl.BlockSpec((1, b, bn), lambda i, j: (i, 0, j)),
        out_shape=jax.ShapeDtypeStruct((n, b, e), F32),
        compiler_params=_params(("arbitrary", "arbitrary")),
        name="ada_mods",
    )(c, ada_w, ada_b)


def _ffn_kernel(x_ref, xn_ref, mod_ref, modn_ref, gpre_ref, gpost_ref, wg_ref, wu_ref, wd_ref, o_ref,
                h_ref, inv_ref, *, res_w, d):
    i = pl.program_id(0)
    f = pl.program_id(1)
    slot = lax.rem(i, 2)

    @pl.when(jnp.logical_and(i == 0, f == 0))
    def _():
        _premod_rows(x_ref, mod_ref, gpre_ref, h_ref.at[0], inv_ref, d)

    @pl.when(f == 0)
    def _():
        o_ref[...] = jnp.zeros_like(o_ref)

    h = h_ref[slot]
    hg = jnp.dot(h, wg_ref[...], preferred_element_type=F32)
    hu = jnp.dot(h, wu_ref[...], preferred_element_type=F32)
    a = (_silu(hg) * hu).astype(BF16)
    _premod_next(xn_ref, modn_ref, gpre_ref, h_ref, 1 - slot, f * xn_ref.shape[0], d)
    o_ref[...] += jnp.dot(a, wd_ref[...], preferred_element_type=F32)

    @pl.when(f == pl.num_programs(1) - 1)
    def _():
        _post_residual_rows(o_ref, x_ref, mod_ref, gpost_ref, o_ref, inv_ref, res_w, d)


def _ffn_sublayer(x, mod, g_pre, g_post, wg, wu, wd, res_w, s, bm, bf, cast=None):
    m, d = x.shape
    f = wd.shape[0]
    tps = s // bm
    n_tiles, nf = m // bm, f // bf
    if wg.ndim == 3:
        assert wg.shape == wu.shape == (nf, d, bf)
        w_in_spec = pl.BlockSpec((None, d, bf), lambda i, j: (j, 0, 0))
    else:
        w_in_spec = pl.BlockSpec((d, bf), lambda i, j: (0, j))
    rps = bm // nf
    assert bm % nf == 0 and rps % min(NORM_ROW_CHUNK, rps) == 0
    nxt = lambda i: jnp.minimum(i + 1, n_tiles - 1)
    return _call_with_cast_job(
        functools.partial(_ffn_kernel, res_w=res_w, d=d),
        cast=cast, cast_grid=(n_tiles, nf), step_of=lambda i, j: (i, j),
        grid=(n_tiles, nf),
        in_specs=[
            pl.BlockSpec((bm, d), lambda i, j: (i, 0)),
            pl.BlockSpec((rps, d), lambda i, j: (nxt(i) * nf + j, 0)),
            pl.BlockSpec((1, 1, 3 * d), lambda i, j: (i // tps, 0, 0)),
            pl.BlockSpec((1, 1, 3 * d), lambda i, j: (nxt(i) // tps, 0, 0)),
            pl.BlockSpec((1, d), lambda i, j: (0, 0)),
            pl.BlockSpec((1, d), lambda i, j: (0, 0)),
            w_in_spec,
            w_in_spec,
            pl.BlockSpec((bf, d), lambda i, j: (j, 0)),
        ],
        out_specs=[pl.BlockSpec((bm, d), lambda i, j: (i, 0))],
        out_shape=[jax.ShapeDtypeStruct((m, d), F32)],
        scratch_shapes=[pltpu.VMEM((2, bm, d), BF16), pltpu.VMEM((bm, LANES), F32)],
        sem=("arbitrary", "arbitrary"),
        name="ffn_sublayer",
        args=(x, x, mod, mod, g_pre, g_post, wg, wu, wd))


def _qkvg_kernel(x0_ref, xn_ref, mod0_ref, modn_ref, gpre_ref, cos_ref, sin_ref, wqk_ref, wv_ref, wg_ref,
                 qk_ref, v_ref, g_ref, h_ref, inv_ref, *, d, dk, n_q):
    i = pl.program_id(0)
    j = pl.program_id(1)
    slot = lax.rem(i, 2)

    @pl.when(jnp.logical_and(i == 0, j == 0))
    def _():
        _premod_rows(x0_ref, mod0_ref, gpre_ref, h_ref.at[0], inv_ref, d)

    h = h_ref[slot]
    g_ref[...] = _silu(jnp.dot(h, wg_ref[...], preferred_element_type=F32)).astype(BF16)
    y = jnp.dot(h, wqk_ref[...], preferred_element_type=F32)
    _premod_next(xn_ref, modn_ref, gpre_ref, h_ref, 1 - slot, j * xn_ref.shape[0], d)
    v_ref[...] = jnp.dot(h, wv_ref[...], preferred_element_type=F32).astype(BF16)

    half = dk // 2
    sc = jnp.where(j >= n_q, dk ** -0.5, 1.0).astype(F32)
    cos = cos_ref[...] * sc
    sin = sin_ref[...] * sc
    for hh in range(y.shape[1] // dk):
        t1 = y[:, hh * dk:hh * dk + half]
        t2 = y[:, hh * dk + half:(hh + 1) * dk]
        qk_ref[:, hh * dk:hh * dk + half] = (t1 * cos - t2 * sin).astype(BF16)
        qk_ref[:, hh * dk + half:(hh + 1) * dk] = (t1 * sin + t2 * cos).astype(BF16)


def _qkvg_proj(x, mod, g_pre, cos, sin, w_in, s, t, cast=None):
    m, d = x.shape
    bm, bn, dk = t["bm_qkvg"], t["bn"], t["dk"]
    tps = s // bm
    vw = 2 * d
    nj = vw // bn
    n_tiles = m // bm
    rps = bm // nj
    assert bm % nj == 0 and rps % min(NORM_ROW_CHUNK, rps) == 0
    nxt = lambda i: jnp.minimum(i + 1, n_tiles - 1)
    half = dk // 2
    col = lambda i, j: (i, j)
    return _call_with_cast_job(
        functools.partial(_qkvg_kernel, d=d, dk=dk, n_q=d // bn),
        cast=cast, cast_grid=(n_tiles, nj), step_of=lambda i, j: (i, j),
        grid=(n_tiles, nj),
        in_specs=[
            _resident((bm, d), lambda i, j: (0, 0)),
            pl.BlockSpec((rps, d), lambda i, j: (nxt(i) * nj + j, 0)),
            _resident((1, 1, 3 * d), lambda i, j: (0, 0, 0)),
            pl.BlockSpec((1, 1, 3 * d), lambda i, j: (nxt(i) // tps, 0, 0)),
            pl.BlockSpec((1, d), lambda i, j: (0, 0)),
            pl.BlockSpec((bm, half), lambda i, j: (i % tps, 0)),
            pl.BlockSpec((bm, half), lambda i, j: (i % tps, 0)),
            pl.BlockSpec((d, bn), lambda i, j: (0, j)),
            pl.BlockSpec((d, bn), lambda i, j: (0, nj + j)),
            pl.BlockSpec((d, bn), lambda i, j: (0, 2 * nj + j)),
        ],
        out_specs=[pl.BlockSpec((bm, bn), col), pl.BlockSpec((bm, bn), col), pl.BlockSpec((bm, bn), col)],
        out_shape=[
            jax.ShapeDtypeStruct((m, 2 * d), BF16),
            jax.ShapeDtypeStruct((m, vw), BF16),
            jax.ShapeDtypeStruct((m, vw), BF16),
        ],
        scratch_shapes=[pltpu.VMEM((2, bm, d), BF16), pltpu.VMEM((bm, LANES), F32)],
        sem=("arbitrary", "arbitrary"),
        name="ret_qkvg_proj",
        args=(x, x, mod, mod, g_pre, cos, sin, w_in, w_in, w_in))


def _retention_kernel(q_ref, k_ref, v_ref, g_ref, gn_ref, dmat_ref, tab_ref, o_ref, state_ref, *, tr):
    tab = tab_ref[0]
    qdec, kdec, cdec = tab[:, 0:1], tab[:, 1:2], tab[0:1, 2:3]
    state_ref[...] = jnp.zeros_like(state_ref)

    for blk in range(q_ref.shape[0] // tr):
        rows = slice(blk * tr, (blk + 1) * tr)
        q = q_ref[rows, :]
        k = k_ref[rows, :]
        v = v_ref[rows, :]
        state = state_ref[...]

        scores = lax.dot_general(q, k, (((1,), (1,)), ((), ())), preferred_element_type=F32)
        p = (scores * dmat_ref[0]).astype(BF16)
        o = jnp.dot(p, v, preferred_element_type=F32)
        o = o + qdec * jnp.dot(q, state.astype(BF16), preferred_element_type=F32)

        kd = (k.astype(F32) * kdec).astype(BF16)
        state_ref[...] = state * cdec + lax.dot_general(
            kd, v, (((0,), (0,)), ((), ())), preferred_element_type=F32)

        mu = jnp.mean(o, axis=-1, keepdims=True)
        oc = o - mu
        var = jnp.mean(oc * oc, axis=-1, keepdims=True)
        on = oc * lax.rsqrt(var + EPS) * gn_ref[...]
        o_ref[rows, :] = (g_ref[rows, :].astype(F32) * on).astype(BF16)


def _retention_core(qk, v, g, gn_g, dmat, tab, b, s, t, cast=None):
    m = qk.shape[0]
    h = RET_HEADS
    tr = t["tr"]
    dk = qk.shape[1] // (2 * h)
    dv = v.shape[1] // h
    return _call_with_cast_job(
        functools.partial(_retention_kernel, tr=tr),
        cast=cast, cast_grid=(b * h, 1), step_of=lambda bi, hi: (bi * h + hi, 0),
        grid=(b, h),
        in_specs=[
            pl.BlockSpec((s, dk), lambda bi, hi: (bi, hi)),
            pl.BlockSpec((s, dk), lambda bi, hi: (bi, h + hi)),
            pl.BlockSpec((s, dv), lambda bi, hi: (bi, hi)),
            pl.BlockSpec((s, dv), lambda bi, hi: (bi, hi)),
            pl.BlockSpec((1, dv), lambda bi, hi: (0, hi)),
            pl.BlockSpec((1, tr, tr), lambda bi, hi: (hi, 0, 0)),
            pl.BlockSpec((1, tr, 128), lambda bi, hi: (hi, 0, 0)),
        ],
        out_specs=[pl.BlockSpec((s, dv), lambda bi, hi: (bi, hi))],
        out_shape=[jax.ShapeDtypeStruct((m, h * dv), BF16)],
        scratch_shapes=[pltpu.VMEM((dk, dv), F32)],
        sem=("parallel", "parallel"),
        name="ret_core",
        args=(qk, qk, v, g, gn_g, dmat, tab))


def _retention_tables(s, dk, tr):
    half = dk // 2
    inv = ROPE_BASE ** (-jnp.arange(half, dtype=F32) / half)
    ang = jnp.arange(s, dtype=F32)[:, None] * inv[None, :]
    gamma = 1.0 - jnp.exp2(-5.0 - jnp.arange(RET_HEADS, dtype=F32))
    log_g = jnp.log(gamma)
    r = jnp.arange(tr, dtype=F32)
    n, mm = r[:, None], r[None, :]
    cn, cm = jnp.floor(n / CHUNK), jnp.floor(mm / CHUNK)
    dmat = jnp.where(cm <= cn, jnp.exp(log_g[:, None, None] * jnp.abs(n - mm)), 0.0)
    tab = jnp.zeros((RET_HEADS, tr, 128), F32)
    tab = tab.at[:, :, 0].set(jnp.exp(log_g[:, None] * (r[None, :] + 1.0)))
    tab = tab.at[:, :, 1].set(jnp.exp(log_g[:, None] * (tr - 1.0 - r[None, :])))
    tab = tab.at[:, :, 2].set(jnp.exp(log_g * tr)[:, None])
    return jnp.cos(ang), jnp.sin(ang), dmat, tab


def _outproj_kernel(y_ref, x_ref, mod_ref, gpost_ref, w_ref, o_ref, inv_ref, *, res_w, d):
    o_ref[...] = jnp.dot(y_ref[...], w_ref[...], preferred_element_type=F32)
    _post_residual_rows(o_ref, x_ref, mod_ref, gpost_ref, o_ref, inv_ref, res_w, d)


def _outproj_sublayer(y, x, mod, g_post, w, res_w, s, t, cast=None):
    m, d = x.shape
    kk = y.shape[1]
    bm = t["bm"]
    tps = s // bm
    return _call_with_cast_job(
        functools.partial(_outproj_kernel, res_w=res_w, d=d),
        cast=cast, cast_grid=(m // bm, 1), step_of=lambda i: (i, 0),
        grid=(m // bm,),
        in_specs=[
            pl.BlockSpec((bm, kk), lambda i: (i, 0)),
            pl.BlockSpec((bm, d), lambda i: (i, 0)),
            pl.BlockSpec((1, 1, 3 * d), lambda i: (i // tps, 0, 0)),
            pl.BlockSpec((1, d), lambda i: (0, 0)),
            _resident((kk, d), lambda i: (0, 0)),
        ],
        out_specs=[pl.BlockSpec((bm, d), lambda i: (i, 0))],
        out_shape=[jax.ShapeDtypeStruct((m, d), F32)],
        scratch_shapes=[pltpu.VMEM((bm, LANES), F32)],
        sem=("parallel",),
        name="ret_outproj",
        args=(y, x, mod, g_post, w))


def _glu_kernel(x0_ref, xn_ref, mod0_ref, modn_ref, gpre_ref, w_ref, b_ref, u_ref, h_ref, inv_ref, *, d, bn):
    i = pl.program_id(0)
    slot = lax.rem(i, 2)

    @pl.when(i == 0)
    def _():
        _premod_rows(x0_ref, mod0_ref, gpre_ref, h_ref.at[0], inv_ref, d)

    h = h_ref[slot]
    for j in range(d // bn):
        ca, cb = slice(j * bn, (j + 1) * bn), slice(d + j * bn, d + (j + 1) * bn)
        a = jnp.dot(h, w_ref[:, ca], preferred_element_type=F32) + b_ref[:, ca]
        b = jnp.dot(h, w_ref[:, cb], preferred_element_type=F32) + b_ref[:, cb]
        u_ref[:, ca] = a * jax.nn.sigmoid(b)

    _premod_next(xn_ref, modn_ref, gpre_ref, h_ref, 1 - slot, 0, d)


def _glu_proj(x, mod, g_pre, w_in, b_in, s, t):
    m, d = x.shape
    bm, bn = t["bm"], t["bn"]
    tps = s // bm
    n_tiles = m // bm
    nxt = lambda i: jnp.minimum(i + 1, n_tiles - 1)
    return pl.pallas_call(
        functools.partial(_glu_kernel, d=d, bn=bn),
        grid=(n_tiles,),
        in_specs=[
            _resident((bm, d), lambda i: (0, 0)),
            pl.BlockSpec((bm, d), lambda i: (nxt(i), 0)),
            _resident((1, 1, 3 * d), lambda i: (0, 0, 0)),
            pl.BlockSpec((1, 1, 3 * d), lambda i: (nxt(i) // tps, 0, 0)),
            pl.BlockSpec((1, d), lambda i: (0, 0)),
            _resident((d, 2 * d), lambda i: (0, 0)),
            pl.BlockSpec((1, 2 * d), lambda i: (0, 0)),
        ],
        out_specs=pl.BlockSpec((bm, d), lambda i: (i, 0)),
        out_shape=jax.ShapeDtypeStruct((m, d), F32),
        scratch_shapes=[pltpu.VMEM((2, bm, d), BF16), pltpu.VMEM((bm, LANES), F32)],
        compiler_params=_params(("arbitrary",)),
        name="conv_glu_proj",
    )(x, x, mod, mod, g_pre, w_in, b_in)


def _conv_kernel(ucur_ref, uprev_ref, x_ref, mod_ref, dw_ref, dwb_ref, lng_ref, lnb_ref, w_ref,
                 bout_ref, gpost_ref, o_ref, ubuf_ref, cbuf_ref, hc_ref, z_ref, inv_ref,
                 *, res_w, d, tps, n_tiles, rc, cc):
    i = pl.program_id(0)
    bm = ucur_ref.shape[0]

    @pl.when(i == 0)
    def _():
        hc_ref[...] = jnp.zeros_like(hc_ref)

    first = (jnp.minimum(i, n_tiles - 1) % tps) == 0
    for c in range(d // cc):
        cols = slice(c * cc, (c + 1) * cc)
        ubuf_ref[c, 0:CONV_HALO, :] = jnp.where(first, 0.0, uprev_ref[:, cols])
        ubuf_ref[c, CONV_HALO:CONV_HALO + bm, :] = ucur_ref[:, cols]
    off = CONV_HALO - (CONV_WIDTH - 1)

    def blk_body(blk, carry):
        base = pl.multiple_of(blk * rc, rc)
        for c in range(d // cc):
            cols = slice(c * cc, (c + 1) * cc)
            acc = jnp.broadcast_to(dwb_ref[:, cols], (rc, cc))
            for tap in range(CONV_WIDTH):
                acc = acc + ubuf_ref[c, pl.ds(base + off + tap, rc), :] * dw_ref[tap:tap + 1, cols]
            cbuf_ref[pl.ds(base, rc), cols] = acc
        z_ref[blk] = jnp.dot(hc_ref[...], w_ref[blk], preferred_element_type=F32)
        return carry

    lax.fori_loop(0, bm // rc, blk_body, 0)

    @pl.when(i > 0)
    def _():
        _post_residual_slabs(z_ref, x_ref, mod_ref, gpost_ref, o_ref, inv_ref, bout_ref, res_w, d)

    cv = cbuf_ref[...]
    mu = jnp.mean(cv, axis=-1, keepdims=True)
    cz = cv - mu
    var = jnp.mean(cz * cz, axis=-1, keepdims=True)
    yn = cz * lax.rsqrt(var + EPS) * lng_ref[...] + lnb_ref[...]
    hc_ref[...] = _silu(yn).astype(BF16)


def _conv_sublayer(u, x, mod, dw, dw_b, ln_g, ln_b, w_out, b_out, g_post, r, res_w, s, t, cast=None):
    m, d = x.shape
    bm = t["bm"]
    tps = s // bm
    n_tiles = m // bm
    hb = bm // CONV_HALO
    rc, cc = min(CONV_ROW_CHUNK, bm), min(CONV_LANE_CHUNK, d)
    n_blk = bm // rc
    assert w_out.shape[1:] == (n_blk, d, d // n_blk)
    full = lambda i: (0, 0)
    cur = lambda i: jnp.minimum(i, n_tiles - 1)
    prev = lambda i: jnp.maximum(i - 1, 0)
    return _call_with_cast_job(
        functools.partial(_conv_kernel, res_w=res_w, d=d, tps=tps, n_tiles=n_tiles, rc=rc, cc=cc),
        cast=cast, cast_grid=(n_tiles, 1), step_of=lambda i: (cur(i), 0),
        grid=(n_tiles + 1,),
        in_specs=[
            pl.BlockSpec((bm, d), lambda i: (cur(i), 0)),
            pl.BlockSpec((CONV_HALO, d), lambda i: (jnp.maximum(cur(i) * hb - 1, 0), 0)),
            pl.BlockSpec((bm, d), lambda i: (prev(i), 0)),
            pl.BlockSpec((1, 1, 3 * d), lambda i: (prev(i) // tps, 0, 0)),
            pl.BlockSpec(dw.shape, full),
            pl.BlockSpec((1, d), full),
            pl.BlockSpec((1, d), full),
            pl.BlockSpec((1, d), full),
            _resident((None, n_blk, d, d // n_blk), lambda i: (r, 0, 0, 0)),
            pl.BlockSpec((1, d), full),
            pl.BlockSpec((1, d), full),
        ],
        out_specs=[pl.BlockSpec((bm, d), lambda i: (prev(i), 0))],
        out_shape=[jax.ShapeDtypeStruct((m, d), F32)],
        scratch_shapes=[
            pltpu.VMEM((d // cc, CONV_HALO + bm, cc), F32),
            pltpu.VMEM((bm, d), F32),
            pltpu.VMEM((bm, d), BF16),
            pltpu.VMEM((n_blk, bm, d // n_blk), F32),
            pltpu.VMEM((bm, LANES), F32),
        ],
        sem=("arbitrary",),
        name="conv_sublayer",
        args=(u, u, x, mod, dw, dw_b, ln_g, ln_b, w_out, b_out, g_post))


def kernel(x, c, ada_w, ada_b, norm_pre, norm_post, ffn_w_gate, ffn_w_up, ffn_w_down, ret_w_in, ret_gn,
           ret_w_out, conv_w_in, conv_b_in, conv_dw, conv_dw_b, conv_ln_g, conv_ln_b, conv_w_out, conv_b_out):
    b, s, d = x.shape
    depth = ada_w.shape[0]
    m = b * s
    f = ffn_w_gate.shape[-1]
    t = _tiles(m, s, d, f)
    assert s % t["bm"] == 0 and s % t["tr"] == 0 and t["tr"] % CHUNK == 0 and t["bm"] % CONV_HALO == 0
    assert CONV_HALO >= CONV_WIDTH - 1 and t["bn"] % t["dk"] == 0

    mods = _ada_mods(c, ada_w.reshape(depth * N_SUBLAYERS, d, 3 * d),
                     ada_b.reshape(depth * N_SUBLAYERS, 1, 3 * d), t)
    mods = mods.reshape(depth, N_SUBLAYERS, b, 1, 3 * d)
    row = lambda p: p.reshape(1, -1)
    cos, sin, dmat, tab = _retention_tables(s, t["dk"], t["tr"])
    dw_pad = CONV_HALO - CONV_WIDTH
    n_blk = t["bm"] // min(CONV_ROW_CHUNK, t["bm"])
    conv_out = conv_w_out.astype(BF16).reshape(-1, d, n_blk, d // n_blk).transpose(0, 2, 1, 3)

    stacked = dict(gate=ffn_w_gate, up=ffn_w_up, down=ffn_w_down, ret_in=ret_w_in, ret_out=ret_w_out,
                   conv_in=conv_w_in)
    ffn_names = ("gate", "up", "down")
    bf16_w = {}

    nf = f // t["bf"]
    col_tiled_names = ("gate", "up")

    def weight(name, lead):
        if (name, lead) not in bf16_w:
            w = stacked[name][lead].astype(BF16)
            if name in col_tiled_names:
                w = w.reshape(d, nf, f // nf).transpose(1, 0, 2)
            bf16_w[name, lead] = w
        return bf16_w[name, lead]

    def jobs(keys, host_col_blocks):
        keys = [k for k in keys if k not in bf16_w and all(a < n for a, n in zip(k[1], stacked[k[0]].shape))]
        return keys, [(stacked[name], lead, name in col_tiled_names and host_col_blocks == nf)
                      for name, lead in keys]

    def store(keys, mats):
        bf16_w.update(zip(keys, mats))

    def ffn(xin, i, j, which, cast_keys):
        keys, cast = jobs(cast_keys, nf)
        (out,), mats = _ffn_sublayer(
            xin, mods[i, j], row(norm_pre[i, j]), row(norm_post[i, j]),
            *[weight(n, (i, which)) for n in ffn_names], 0.5, s, t["bm"], t["bf"], cast=cast)
        store(keys, mats)
        return out

    ffn_keys = lambda i, which: [(n, (i, which)) for n in ffn_names]
    xf = x.reshape(m, d)
    for i in range(depth):
        r = i // 2
        if i == 0:
            first_jobs = [(n, (k,)) for n in ("ret_in", "ret_out", "conv_in") for k in range(stacked[n].shape[0])]
        else:
            first_jobs = []
        if i % 2 == 1:
            first_jobs += ffn_keys(i, 1)
        xf = ffn(xf, i, 0, 0, first_jobs)
        if i % 2 == 0:
            keys, cast = jobs(ffn_keys(i, 1), 2 * d // t["bn"])
            (qk, v, g), mats = _qkvg_proj(xf, mods[i, 1], row(norm_pre[i, 1]), cos, sin, weight("ret_in", (r,)),
                                          s, t, cast=cast)
            store(keys, mats)
            (y,), _ = _retention_core(qk, v, g, row(ret_gn[r]), dmat, tab, b, s, t)
            keys, cast = jobs(ffn_keys(i + 1, 0), 1)
            (xf,), mats = _outproj_sublayer(y, xf, mods[i, 1], row(norm_post[i, 1]), weight("ret_out", (r,)),
                                            1.0, s, t, cast=cast)
            store(keys, mats)
        else:
            u = _glu_proj(xf, mods[i, 1], row(norm_pre[i, 1]), weight("conv_in", (r,)), row(conv_b_in[r]), s, t)
            (xf,), _ = _conv_sublayer(
                u, xf, mods[i, 1], jnp.pad(conv_dw[r], ((0, dw_pad), (0, 0))), row(conv_dw_b[r]),
                row(conv_ln_g[r]), row(conv_ln_b[r]), conv_out, row(conv_b_out[r]), row(norm_post[i, 1]),
                r, 1.0, s, t)
        xf = ffn(xf, i, 2, 1, [])
    return xf.reshape(b, s, d)
```
